```python
import jax, jax.numpy as jnp
from jax import lax
import numpy as np

D_MODEL = 2048
BATCH = 8
SEQ = 4096
DEPTH = 4

CHUNK = 64
N_A_LAYERS = DEPTH // 2
N_B_LAYERS = DEPTH - N_A_LAYERS
GMLP_BLOCK = 128
GMLP_HALF = 3 * D_MODEL
GMLP_GROUPS = 8
GMLP_GROUP_DIM = GMLP_HALF // GMLP_GROUPS
N_HEADS = 16
HEAD_DIM = D_MODEL // N_HEADS
Q_BLOCK = 128
D_FF = 4 * D_MODEL
N_MOD = 6
EPS = 1e-6

kernel_name = "yoco_gmlp_forgetting_attn_adaln_trunk"


def rms_norm(x, g):
    xf = x.astype(jnp.float32)
    y = xf * lax.rsqrt(jnp.mean(xf * xf, axis=-1, keepdims=True) + EPS)
    return y.astype(x.dtype) * g


def layer_norm(x, g, b):
    xf = x.astype(jnp.float32)
    mu = jnp.mean(xf, axis=-1, keepdims=True)
    var = jnp.mean(jnp.square(xf - mu), axis=-1, keepdims=True)
    y = (xf - mu) * lax.rsqrt(var + EPS)
    return y.astype(x.dtype) * g + b


def modulate(h, shift, scale):
    return h * (1.0 + scale[:, None, :]) + shift[:, None, :]


def split_heads(t):
    b, s, _ = t.shape
    return t.reshape(b, s, N_HEADS, HEAD_DIM)


def gmlp_block_mask():
    idx = np.arange(GMLP_BLOCK) // CHUNK
    return jnp.asarray(idx[None, :] <= idx[:, None])


def gmlp_mixer(h, w_in, ln_g, ln_b, ws, bs, w_out):
    b, s, _ = h.shape
    z = jax.nn.gelu(h @ w_in, approximate=False)
    u, v = jnp.split(z, 2, axis=-1)
    v = layer_norm(v, ln_g, ln_b)
    n_blk = s // GMLP_BLOCK
    v = v.reshape(b, n_blk, GMLP_BLOCK, GMLP_GROUPS, GMLP_GROUP_DIM)
    w = jnp.where(gmlp_block_mask()[None], ws, jnp.zeros((), ws.dtype))
    sv = jnp.einsum('gij,bnjgc->bnigc', w, v) + bs.T[:, :, None]
    sv = sv.reshape(b, s, GMLP_HALF)
    return (u * sv) @ w_out


def squared_relu_mlp(h, w1, w2):
    return jnp.square(jax.nn.relu(h @ w1)) @ w2


def shared_kv(x, sc, kv_norm_g, kv_ada_w, kv_ada_b, w_kv, k_norm_g, w_f, b_f):
    shift, scale = jnp.split(sc @ kv_ada_w + kv_ada_b, 2, axis=-1)
    h = modulate(rms_norm(x, kv_norm_g), shift, scale)
    k, v = jnp.split(h @ w_kv, 2, axis=-1)
    k = rms_norm(split_heads(k), k_norm_g).transpose(0, 2, 1, 3)
    v = split_heads(v).transpose(0, 2, 1, 3)
    logf = jax.nn.log_sigmoid((h @ w_f).astype(jnp.float32) + b_f.astype(jnp.float32))
    fcum = jnp.cumsum(logf, axis=1).transpose(0, 2, 1)
    return k, v, fcum


def forgetting_attention(q, k, v, fcum):
    b, nh, s, dh = q.shape
    n_blk = s // Q_BLOCK
    qb = q.reshape(b, nh, n_blk, Q_BLOCK, dh).transpose(2, 0, 1, 3, 4)
    fq = fcum.reshape(b, nh, n_blk, Q_BLOCK).transpose(2, 0, 1, 3)
    key_pos = jnp.arange(s)
    inv_sqrt = 1.0 / float(np.sqrt(dh))

    def one_block(args):
        i, q_i, f_i = args
        logits = jnp.einsum('bhqd,bhkd->bhqk', q_i, k,
                            preferred_element_type=jnp.float32) * inv_sqrt
        logits = logits + (f_i[..., :, None] - fcum[..., None, :])
        q_pos = i * Q_BLOCK + jnp.arange(Q_BLOCK)
        logits = jnp.where(key_pos[None, :] <= q_pos[:, None], logits, -jnp.inf)
        p = jax.nn.softmax(logits, axis=-1)
        return jnp.einsum('bhqk,bhkd->bhqd', p.astype(v.dtype), v)

    out = lax.map(one_block, (jnp.arange(n_blk), qb, fq))
    return out.transpose(1, 2, 0, 3, 4).reshape(b, nh, s, dh)


def attention_mixer(h, k, v, fcum, wq, q_norm_g, wo):
    b, s, _ = h.shape
    q = rms_norm(split_heads(h @ wq), q_norm_g).transpose(0, 2, 1, 3)
    o = forgetting_attention(q, k, v, fcum)
    return o.transpose(0, 2, 1, 3).reshape(b, s, D_MODEL) @ wo


def setup_inputs(seed: int = 0) -> dict:
    key = jax.random.key(seed)
    ks = jax.random.split(key, 24)
    f32 = jnp.float32
    nrm = lambda k, shape, s: jax.random.normal(k, shape, f32) * s
    d = D_MODEL
    return {
        "x": nrm(ks[0], (BATCH, SEQ, d), 1.0),
        "c": nrm(ks[1], (BATCH, d), 1.0),
        "ada_w": nrm(ks[2], (DEPTH, d, N_MOD * d), d ** -0.5),
        "ada_b": nrm(ks[3], (DEPTH, N_MOD * d), 0.02),
        "norm_g": 1.0 + nrm(ks[4], (DEPTH, 2, d), 0.02),
        "mlp_w1": nrm(ks[5], (DEPTH, d, D_FF), d ** -0.5),
        "mlp_w2": nrm(ks[6], (DEPTH, D_FF, d), D_FF ** -0.5),
        "gmlp_w_in": nrm(ks[7], (N_A_LAYERS, d, 2 * GMLP_HALF), d ** -0.5),
        "gmlp_ln_g": 1.0 + nrm(ks[8], (N_A_LAYERS, GMLP_HALF), 0.02),
        "gmlp_ln_b": nrm(ks[9], (N_A_LAYERS, GMLP_HALF), 0.02),
        "gmlp_ws": nrm(ks[10], (N_A_LAYERS, GMLP_GROUPS, GMLP_BLOCK, GMLP_BLOCK), GMLP_BLOCK ** -0.5),
        "gmlp_bs": 1.0 + nrm(ks[11], (N_A_LAYERS, GMLP_GROUPS, GMLP_BLOCK), 0.1),
        "gmlp_w_out": nrm(ks[12], (N_A_LAYERS, GMLP_HALF, d), GMLP_HALF ** -0.5),
        "kv_norm_g": 1.0 + nrm(ks[13], (d,), 0.02),
        "kv_ada_w": nrm(ks[14], (d, 2 * d), d ** -0.5),
        "kv_ada_b": nrm(ks[15], (2 * d,), 0.02),
        "w_kv": nrm(ks[16], (d, 2 * d), d ** -0.5),
        "k_norm_g": 1.0 + nrm(ks[17], (HEAD_DIM,), 0.02),
        "w_f": nrm(ks[18], (d, N_HEADS), 0.1 * d ** -0.5),
        "b_f": jax.random.uniform(ks[19], (N_HEADS,), f32, 0.5, 5.0),
        "attn_wq": nrm(ks[20], (N_B_LAYERS, d, d), d ** -0.5),
        "q_norm_g": 1.0 + nrm(ks[21], (N_B_LAYERS, HEAD_DIM), 0.02),
        "attn_wo": nrm(ks[22], (N_B_LAYERS, d, d), d ** -0.5),
    }


def reference(x, c, ada_w, ada_b, norm_g, mlp_w1, mlp_w2, gmlp_w_in, gmlp_ln_g,
              gmlp_ln_b, gmlp_ws, gmlp_bs, gmlp_w_out, kv_norm_g, kv_ada_w, kv_ada_b,
              w_kv, k_norm_g, w_f, b_f, attn_wq, q_norm_g, attn_wo):
    sc = jax.nn.silu(c)
    k = v = fcum = None
    for layer in range(DEPTH):
        mod = sc @ ada_w[layer] + ada_b[layer]
        sh1, sc1, g1, sh2, sc2, g2 = jnp.split(mod, N_MOD, axis=-1)
        h = modulate(rms_norm(x, norm_g[layer, 0]), sh1, sc1)
        if layer < N_A_LAYERS:
            a = layer
            y = gmlp_mixer(h, gmlp_w_in[a], gmlp_ln_g[a], gmlp_ln_b[a],
                           gmlp_ws[a], gmlp_bs[a], gmlp_w_out[a])
        else:
            if layer == N_A_LAYERS:
                k, v, fcum = shared_kv(x, sc, kv_norm_g, kv_ada_w, kv_ada_b,
                                       w_kv, k_norm_g, w_f, b_f)
            bl = layer - N_A_LAYERS
            y = attention_mixer(h, k, v, fcum, attn_wq[bl], q_norm_g[bl], attn_wo[bl])
        x = x + g1[:, None, :] * y
        h = modulate(rms_norm(x, norm_g[layer, 1]), sh2, sc2)
        x = x + g2[:, None, :] * squared_relu_mlp(h, mlp_w1[layer], mlp_w2[layer])
    return x
```

```python
import functools

import jax
import jax.numpy as jnp
from jax import lax
from jax.experimental import pallas as pl
from jax.experimental.pallas import tpu as pltpu

F32 = jnp.float32
BF16 = jnp.bfloat16

EPS = 1e-6
CHUNK = 64
GMLP_BLOCK = 128
GMLP_GROUPS = 8
HEAD_DIM = 128
N_MOD = 6
LANES = 128

MIB = 1024 * 1024
VMEM_LIMIT = 52 * MIB


def _params(semantics, vmem=VMEM_LIMIT):
    return pltpu.CompilerParams(dimension_semantics=semantics, vmem_limit_bytes=vmem)


def _adaln(x, g, shift, scale):
    ms = jnp.mean(x * x, axis=-1, keepdims=True)
    y = x * lax.rsqrt(ms + EPS) * g
    return y * (1.0 + scale) + shift


def _gelu(a):
    return 0.5 * a * (1.0 + lax.erf(a * (0.5 ** 0.5)))


def _mod_kernel(c_ref, w_ref, b_ref, o_ref):
    sc = jax.nn.silu(c_ref[...])
    o_ref[0] = jnp.dot(sc, w_ref[0], preferred_element_type=F32) + b_ref[0]


def _mod(c, w, b, tn=1024):
    nl, d, n = w.shape
    nb = c.shape[0]
    return pl.pallas_call(
        _mod_kernel,
        grid=(nl, n // tn),
        in_specs=[
            pl.BlockSpec((nb, d), lambda l, j: (0, 0)),
            pl.BlockSpec((1, d, tn), lambda l, j: (l, 0, j)),
            pl.BlockSpec((1, 1, tn), lambda l, j: (l, 0, j)),
        ],
        out_specs=pl.BlockSpec((1, nb, tn), lambda l, j: (l, 0, j)),
        out_shape=jax.ShapeDtypeStruct((nl, nb, n), F32),
        compiler_params=_params(("arbitrary", "arbitrary")),
        name="mod",
    )(c, w, b.reshape(nl, 1, n))


def _prologue(x_ref, g_ref, sh_ref, sc_ref, h_scr):
    @pl.when(pl.program_id(1) == 0)
    def _():
        h = _adaln(x_ref[...], g_ref[...], sh_ref[0], sc_ref[0])
        h_scr[...] = h.astype(BF16)


def _proj_gelu_kernel(x_ref, g_ref, sh_ref, sc_ref, w_ref, o_ref, h_scr):
    _prologue(x_ref, g_ref, sh_ref, sc_ref, h_scr)
    a = jnp.dot(h_scr[...], w_ref[...], preferred_element_type=F32)
    o_ref[...] = _gelu(a).astype(o_ref.dtype)


def _proj_heads_kernel(x_ref, g_ref, sh_ref, sc_ref, w_ref, hg_ref, o_ref, h_scr, *,
                       norm, out_scale):
    _prologue(x_ref, g_ref, sh_ref, sc_ref, h_scr)
    a = jnp.dot(h_scr[...], w_ref[...], preferred_element_type=F32)
    for hh in range(a.shape[1] // HEAD_DIM):
        t = a[:, hh * HEAD_DIM:(hh + 1) * HEAD_DIM]
        if norm:
            ms = jnp.mean(t * t, axis=-1, keepdims=True)
            t = t * lax.rsqrt(ms + EPS) * hg_ref[...]
        if out_scale != 1.0:
            t = t * out_scale
        o_ref[0, hh] = t.astype(o_ref.dtype)


def _proj_forget_kernel(x_ref, g_ref, sh_ref, sc_ref, w_ref, b_ref, o_ref, *, n_heads):
    h = _adaln(x_ref[...], g_ref[...], sh_ref[0], sc_ref[0]).astype(BF16)
    f = jnp.dot(h, w_ref[...], preferred_element_type=F32) + b_ref[...]
    o_ref[0] = jax.nn.log_sigmoid(f).T[:n_heads]


def _mod_spec(tiles_per_batch, row_of):
    return lambda i, j: (row_of(i // tiles_per_batch), 0, 0)


def _proj_common_specs(d, tm, tpb, sh_row, sc_row):
    return [
        pl.BlockSpec((tm, d), lambda i, j: (i, 0)),
        pl.BlockSpec((1, d), lambda i, j: (0, 0)),
        pl.BlockSpec((1, 1, d), _mod_spec(tpb, sh_row)),
        pl.BlockSpec((1, 1, d), _mod_spec(tpb, sc_row)),
    ]


def _proj_gelu(x2, g, modt, sh_row, sc_row, w, seq, tm=1024, tn=1024):
    t, d = x2.shape
    n = w.shape[1]
    tpb = seq // tm
    return pl.pallas_call(
        _proj_gelu_kernel,
        grid=(t // tm, n // tn),
        in_specs=_proj_common_specs(d, tm, tpb, sh_row, sc_row) + [
            pl.BlockSpec((d, tn), lambda i, j: (0, j)),
        ],
        out_specs=pl.BlockSpec((tm, tn), lambda i, j: (i, j)),
        out_shape=jax.ShapeDtypeStruct((t, n), BF16),
        scratch_shapes=[pltpu.VMEM((tm, d), BF16)],
        compiler_params=_params(("arbitrary", "arbitrary")),
        name="proj_gelu",
    )(x2, g, modt, modt, w)


def _proj_heads(x2, g, modt, sh_row, sc_row, w, head_g, seq, *, norm, out_scale=1.0,
                tm=1024, tn=1024):
    t, d = x2.shape
    n = w.shape[1]
    tpb = seq // tm
    hpt = tn // HEAD_DIM
    kern = functools.partial(_proj_heads_kernel, norm=norm, out_scale=out_scale)
    return pl.pallas_call(
        kern,
        grid=(t // tm, n // tn),
        in_specs=_proj_common_specs(d, tm, tpb, sh_row, sc_row) + [
            pl.BlockSpec((d, tn), lambda i, j: (0, j)),
            pl.BlockSpec((1, HEAD_DIM), lambda i, j: (0, 0)),
        ],
        out_specs=pl.BlockSpec((1, hpt, tm, HEAD_DIM),
                               lambda i, j: (i // tpb, j, i % tpb, 0)),
        out_shape=jax.ShapeDtypeStruct((t // seq, n // HEAD_DIM, seq, HEAD_DIM), BF16),
        scratch_shapes=[pltpu.VMEM((tm, d), BF16)],
        compiler_params=_params(("arbitrary", "arbitrary")),
        name="proj_heads",
    )(x2, g, modt, modt, w, head_g)


def _proj_forget(x2, g, modt, sh_row, sc_row, w_pad, b_pad, n_heads, seq, tm=1024):
    t, d = x2.shape
    tpb = seq // tm
    kern = functools.partial(_proj_forget_kernel, n_heads=n_heads)
    return pl.pallas_call(
        kern,
        grid=(t // tm, 1),
        in_specs=_proj_common_specs(d, tm, tpb, sh_row, sc_row) + [
            pl.BlockSpec((d, LANES), lambda i, j: (0, 0)),
            pl.BlockSpec((1, LANES), lambda i, j: (0, 0)),
        ],
        out_specs=pl.BlockSpec((1, n_heads, tm), lambda i, j: (i // tpb, 0, i % tpb)),
        out_shape=jax.ShapeDtypeStruct((t // seq, n_heads, seq), F32),
        compiler_params=_params(("arbitrary", "arbitrary")),
        name="proj_forget",
    )(x2, g, modt, modt, w_pad, b_pad)


def _gate_kernel(u_ref, v_ref, lg_ref, lb_ref, ws_ref, bst_ref, o_ref, vn_scr):
    v = v_ref[...].astype(F32)
    mu = jnp.mean(v, axis=-1, keepdims=True)
    vc = v - mu
    var = jnp.mean(vc * vc, axis=-1, keepdims=True)
    vn_scr[...] = (vc * lax.rsqrt(var + EPS) * lg_ref[...] + lb_ref[...]).astype(BF16)

    row = lax.broadcasted_iota(jnp.int32, (GMLP_BLOCK, GMLP_BLOCK), 0) // CHUNK
    col = lax.broadcasted_iota(jnp.int32, (GMLP_BLOCK, GMLP_BLOCK), 1) // CHUNK
    keep = col <= row
    gdim = v.shape[1] // GMLP_GROUPS
    for gi in range(GMLP_GROUPS):
        w = jnp.where(keep, ws_ref[gi], 0.0).astype(BF16)
        bias = bst_ref[:, gi:gi + 1]
        cols = slice(gi * gdim, (gi + 1) * gdim)
        for nb in range(v.shape[0] // GMLP_BLOCK):
            rows = slice(nb * GMLP_BLOCK, (nb + 1) * GMLP_BLOCK)
            sv = jnp.dot(w, vn_scr[rows, cols], preferred_element_type=F32) + bias
            o_ref[rows, cols] = (u_ref[rows, cols].astype(F32) * sv).astype(o_ref.dtype)


def _gate(z, ln_g, ln_b, ws, bs_t, tm=256):
    t, n2 = z.shape
    half = n2 // 2
    return pl.pallas_call(
        _gate_kernel,
        grid=(t // tm,),
        in_specs=[
            pl.BlockSpec((tm, half), lambda i: (i, 0)),
            pl.BlockSpec((tm, half), lambda i: (i, 1)),
            pl.BlockSpec((1, half), lambda i: (0, 0)),
            pl.BlockSpec((1, half), lambda i: (0, 0)),
            pl.BlockSpec(ws.shape, lambda i: (0, 0, 0)),
            pl.BlockSpec(bs_t.shape, lambda i: (0, 0)),
        ],
        out_specs=pl.BlockSpec((tm, half), lambda i: (i, 0)),
        out_shape=jax.ShapeDtypeStruct((t, half), BF16),
        scratch_shapes=[pltpu.VMEM((tm, half), BF16)],
        compiler_params=_params(("arbitrary",)),
        name="gate",
    )(z, z, ln_g, ln_b, ws, bs_t)


def _accumulate_gated(o_ref, x_ref, gate_ref, d, k, nk):
    if nk == 1:
        o_ref[...] = x_ref[...] + gate_ref[0] * d
        return

    @pl.when(k == 0)
    def _():
        o_ref[...] = d

    @pl.when(jnp.logical_and(k > 0, k < nk - 1))
    def _():
        o_ref[...] += d

    @pl.when(k == nk - 1)
    def _():
        o_ref[...] = x_ref[...] + gate_ref[0] * (o_ref[...] + d)


def _outproj_kernel(a_ref, w_ref, x_ref, gate_ref, o_ref, *, nk):
    d = jnp.dot(a_ref[...], w_ref[...], preferred_element_type=F32)
    _accumulate_gated(o_ref, x_ref, gate_ref, d, pl.program_id(1), nk)


def _outproj(a, w, x2, modt, gate_row, seq, tm=512, tk=1024):
    t, kdim = a.shape
    d = w.shape[1]
    tpb = seq // tm
    nk = kdim // tk
    return pl.pallas_call(
        functools.partial(_outproj_kernel, nk=nk),
        grid=(t // tm, nk),
        in_specs=[
            pl.BlockSpec((tm, tk), lambda i, k: (i, k)),
            pl.BlockSpec((tk, d), lambda i, k: (k, 0)),
            pl.BlockSpec((tm, d), lambda i, k: (i, 0)),
            pl.BlockSpec((1, 1, d), _mod_spec(tpb, gate_row)),
        ],
        out_specs=pl.BlockSpec((tm, d), lambda i, k: (i, 0)),
        out_shape=jax.ShapeDtypeStruct((t, d), F32),
        compiler_params=_params(("arbitrary", "arbitrary")),
        name="outproj",
    )(a, w, x2, modt)


def _mlp_kernel(x_ref, g_ref, sh_ref, sc_ref, gate_ref, w1_ref, w2_ref, o_ref, h_scr, *, nj):
    _prologue(x_ref, g_ref, sh_ref, sc_ref, h_scr)
    a = jnp.dot(h_scr[...], w1_ref[...], preferred_element_type=F32)
    a = jnp.square(jnp.maximum(a, 0.0)).astype(BF16)
    d = jnp.dot(a, w2_ref[...], preferred_element_type=F32)
    _accumulate_gated(o_ref, x_ref, gate_ref, d, pl.program_id(1), nj)


def _mlp(x2, g, modt, sh_row, sc_row, gate_row, w1, w2, seq, tm=512, tf=1024):
    t, d = x2.shape
    dff = w1.shape[1]
    tpb = seq // tm
    nj = dff // tf
    return pl.pallas_call(
        functools.partial(_mlp_kernel, nj=nj),
        grid=(t // tm, nj),
        in_specs=_proj_common_specs(d, tm, tpb, sh_row, sc_row) + [
            pl.BlockSpec((1, 1, d), _mod_spec(tpb, gate_row)),
            pl.BlockSpec((d, tf), lambda i, j: (0, j)),
            pl.BlockSpec((tf, d), lambda i, j: (j, 0)),
        ],
        out_specs=pl.BlockSpec((tm, d), lambda i, j: (i, 0)),
        out_shape=jax.ShapeDtypeStruct((t, d), F32),
        scratch_shapes=[pltpu.VMEM((tm, d), BF16)],
        compiler_params=_params(("arbitrary", "arbitrary")),
        name="mlp",
    )(x2, g, modt, modt, modt, w1, w2)


def _cumsum_kernel(x_ref, o_ref):
    x = x_ref[...]
    n = x.shape[1]
    lane = lax.broadcasted_iota(jnp.int32, x.shape, 1)
    d = 1
    while d < n:
        x = x + jnp.where(lane >= d, pltpu.roll(x, d, axis=1), 0.0)
        d *= 2
    o_ref[...] = x


def _cumsum(x):
    return pl.pallas_call(
        _cumsum_kernel,
        out_shape=jax.ShapeDtypeStruct(x.shape, x.dtype),
        name="cumsum",
    )(x)


def _flash_kernel(q_ref, k_ref, v_ref, f_ref, o_ref, m_scr, acc_scr, *, tq):
    qi = pl.program_id(2)
    q = q_ref[0, 0]
    f_tile0 = f_ref[0, 0, qi][:, 0:1]
    ones = jnp.ones((tq, HEAD_DIM), BF16)

    m_scr[...] = jnp.full(m_scr.shape, -jnp.inf, F32)
    acc_scr[...] = jnp.zeros(acc_scr.shape, F32)

    def chunk(c, masked):
        start = pl.multiple_of(c * tq, tq)
        kc = k_ref[0, 0, pl.ds(start, tq), :]
        vc = v_ref[0, 0, pl.ds(start, tq), :]
        s = lax.dot_general(q, kc, (((1,), (1,)), ((), ())), preferred_element_type=F32)
        s = s + (f_tile0 - f_ref[0, 0, c])
        if masked:
            row = lax.broadcasted_iota(jnp.int32, s.shape, 0)
            col = lax.broadcasted_iota(jnp.int32, s.shape, 1)
            s = jnp.where(col <= row, s, -jnp.inf)
        m_old = m_scr[...]
        m_new = jnp.maximum(m_old, jnp.max(s, axis=-1, keepdims=True))
        p = jnp.exp(s - m_new).astype(BF16)
        alpha = jnp.exp(m_old - m_new)
        v_aug = jnp.concatenate([vc, ones], axis=1)
        acc_scr[...] = alpha * acc_scr[...] + jnp.dot(p, v_aug, preferred_element_type=F32)
        m_scr[...] = m_new

    def body(c, carry):
        chunk(c, masked=False)
        return carry

    lax.fori_loop(0, qi, body, 0)
    chunk(qi, masked=True)

    acc = acc_scr[...]
    o_ref[0] = (acc[:, :HEAD_DIM] / acc[:, HEAD_DIM:]).astype(o_ref.dtype)


def _flash(q, k, v, fcum, tq=512):
    nb, nh, seq, dh = q.shape
    nq = seq // tq
    f5 = fcum.reshape(nb, nh, nq, 1, tq)
    return pl.pallas_call(
        functools.partial(_flash_kernel, tq=tq),
        grid=(nb, nh, nq),
        in_specs=[
            pl.BlockSpec((1, 1, tq, dh), lambda b, h, i: (b, h, i, 0)),
            pl.BlockSpec((1, 1, seq, dh), lambda b, h, i: (b, h, 0, 0)),
            pl.BlockSpec((1, 1, seq, dh), lambda b, h, i: (b, h, 0, 0)),
            pl.BlockSpec((1, 1, nq, 1, tq), lambda b, h, i: (b, h, 0, 0, 0)),
        ],
        out_specs=pl.BlockSpec((1, tq, dh), lambda b, h, i: (b, i, h)),
        out_shape=jax.ShapeDtypeStruct((nb, seq, nh * dh), BF16),
        scratch_shapes=[pltpu.VMEM((tq, 1), F32), pltpu.VMEM((tq, 2 * dh), F32)],
        compiler_params=_params(("arbitrary", "arbitrary", "arbitrary")),
        name="flash",
    )(q, k, v, f5)


def kernel(x, c, ada_w, ada_b, norm_g, mlp_w1, mlp_w2, gmlp_w_in, gmlp_ln_g, gmlp_ln_b,
           gmlp_ws, gmlp_bs, gmlp_w_out, kv_norm_g, kv_ada_w, kv_ada_b, w_kv, k_norm_g,
           w_f, b_f, attn_wq, q_norm_g, attn_wo):
    nb, seq, d = x.shape
    depth = ada_w.shape[0]
    n_a = gmlp_w_in.shape[0]
    n_heads = w_f.shape[1]
    half = gmlp_w_out.shape[1]

    modt = _mod(c, ada_w, ada_b).reshape(depth * nb * N_MOD, 1, d)
    kvmod = _mod(c, kv_ada_w[None], kv_ada_b[None]).reshape(nb * 2, 1, d)

    def mod_row(layer, which):
        return lambda b: (layer * nb + b) * N_MOD + which

    x2 = x.reshape(nb * seq, d)
    inv_sqrt = 1.0 / float(HEAD_DIM) ** 0.5
    k_heads = v_heads = fcum = None
    for layer in range(depth):
        g1 = norm_g[layer, 0][None]
        g2 = norm_g[layer, 1][None]
        if layer < n_a:
            a = layer
            z = _proj_gelu(x2, g1, modt, mod_row(layer, 0), mod_row(layer, 1),
                           gmlp_w_in[a].astype(BF16), seq)
            p = _gate(z, gmlp_ln_g[a][None], gmlp_ln_b[a][None], gmlp_ws[a], gmlp_bs[a].T)
            x2 = _outproj(p, gmlp_w_out[a].astype(BF16), x2, modt, mod_row(layer, 2), seq)
        else:
            if layer == n_a:
                kv_sh = lambda b: b * 2
                kv_sc = lambda b: b * 2 + 1
                kvg = kv_norm_g[None]
                k_heads = _proj_heads(x2, kvg, kvmod, kv_sh, kv_sc,
                                      w_kv[:, :d].astype(BF16), k_norm_g[None], seq, norm=True)
                v_heads = _proj_heads(x2, kvg, kvmod, kv_sh, kv_sc,
                                      w_kv[:, d:].astype(BF16), k_norm_g[None], seq, norm=False)
                w_f_pad = jnp.pad(w_f, ((0, 0), (0, LANES - n_heads))).astype(BF16)
                b_f_pad = jnp.pad(b_f, (0, LANES - n_heads))[None]
                logf = _proj_forget(x2, kvg, kvmod, kv_sh, kv_sc, w_f_pad, b_f_pad,
                                    n_heads, seq)
                fcum = _cumsum(logf.reshape(nb * n_heads, seq)).reshape(nb, n_heads, seq)
            bl = layer - n_a
            q_heads = _proj_heads(x2, g1, modt, mod_row(layer, 0), mod_row(layer, 1),
                                  attn_wq[bl].astype(BF16), q_norm_g[bl][None], seq,
                                  norm=True, out_scale=inv_sqrt)
            o = _flash(q_heads, k_heads, v_heads, fcum)
            x2 = _outproj(o.reshape(nb * seq, d), attn_wo[bl].astype(BF16), x2, modt,
                          mod_row(layer, 2), seq)
        x2 = _mlp(x2, g2, modt, mod_row(layer, 3), mod_row(layer, 4), mod_row(layer, 5),
                  mlp_w1[layer].astype(BF16), mlp_w2[layer].astype(BF16), seq)
    return x2.reshape(nb, seq, d)
```

```python
import functools

import jax
import jax.numpy as jnp
from jax import lax
from jax.experimental import pallas as pl
from jax.experimental.pallas import tpu as pltpu

F32 = jnp.float32
BF16 = jnp.bfloat16

EPS = 1e-6
CHUNK = 64
GMLP_BLOCK = 128
GMLP_GROUPS = 8
HEAD_DIM = 128
N_MOD = 6
LANES = 128
SUBLANES = 8
LOG2E = 1.4426950408889634
FLASH_TILE = 512

MIB = 1024 * 1024
VMEM_LIMIT = 52 * MIB


def _params(semantics, vmem=VMEM_LIMIT):
    return pltpu.CompilerParams(dimension_semantics=semantics, vmem_limit_bytes=vmem)


def _adaln(x, g, shift, scale):
    ms = jnp.mean(x * x, axis=-1, keepdims=True)
    y = x * lax.rsqrt(ms + EPS) * g
    return y * (1.0 + scale) + shift


def _gelu(a):
    return 0.5 * a * (1.0 + lax.erf(a * (0.5 ** 0.5)))


def _mod_kernel(c_ref, w_ref, b_ref, o_ref):
    sc = jax.nn.silu(c_ref[...])
    o_ref[0] = jnp.dot(sc, w_ref[0], preferred_element_type=F32) + b_ref[0]


def _mod(c, w, b, tn=1024):
    nl, d, n = w.shape
    nb = c.shape[0]
    return pl.pallas_call(
        _mod_kernel,
        grid=(nl, n // tn),
        in_specs=[
            pl.BlockSpec((nb, d), lambda l, j: (0, 0)),
            pl.BlockSpec((1, d, tn), lambda l, j: (l, 0, j)),
            pl.BlockSpec((1, 1, tn), lambda l, j: (l, 0, j)),
        ],
        out_specs=pl.BlockSpec((1, nb, tn), lambda l, j: (l, 0, j)),
        out_shape=jax.ShapeDtypeStruct((nl, nb, n), F32),
        compiler_params=_params(("arbitrary", "arbitrary")),
        name="mod",
    )(c, w, b.reshape(nl, 1, n))


def _prologue(x_ref, g_ref, sh_ref, sc_ref, h_scr):
    @pl.when(pl.program_id(1) == 0)
    def _():
        h = _adaln(x_ref[...], g_ref[...], sh_ref[0], sc_ref[0])
        h_scr[...] = h.astype(BF16)


def _proj_gelu_kernel(x_ref, g_ref, sh_ref, sc_ref, w_ref, o_ref, h_scr):
    _prologue(x_ref, g_ref, sh_ref, sc_ref, h_scr)
    a = jnp.dot(h_scr[...], w_ref[...], preferred_element_type=F32)
    o_ref[...] = _gelu(a).astype(o_ref.dtype)


def _proj_heads_kernel(x_ref, g_ref, sh_ref, sc_ref, w_ref, hg_ref, o_ref, h_scr, *,
                       norm, out_scale, transpose_chunk):
    _prologue(x_ref, g_ref, sh_ref, sc_ref, h_scr)
    a = jnp.dot(h_scr[...], w_ref[...], preferred_element_type=F32)
    for hh in range(a.shape[1] // HEAD_DIM):
        t = a[:, hh * HEAD_DIM:(hh + 1) * HEAD_DIM]
        if norm:
            ms = jnp.mean(t * t, axis=-1, keepdims=True)
            t = t * lax.rsqrt(ms + EPS) * hg_ref[...]
        if out_scale != 1.0:
            t = t * out_scale
        if transpose_chunk:
            tt = t.T
            for cc in range(t.shape[0] // transpose_chunk):
                cols = slice(cc * transpose_chunk, (cc + 1) * transpose_chunk)
                o_ref[0, hh, cc] = tt[:, cols].astype(o_ref.dtype)
        else:
            o_ref[0, hh] = t.astype(o_ref.dtype)


def _proj_forget_kernel(x_ref, g_ref, sh_ref, sc_ref, w_ref, b_ref, o_ref, *, n_heads):
    h = _adaln(x_ref[...], g_ref[...], sh_ref[0], sc_ref[0]).astype(BF16)
    f = jnp.dot(h, w_ref[...], preferred_element_type=F32) + b_ref[...]
    o_ref[0] = jax.nn.log_sigmoid(f).T[:n_heads]


def _mod_spec(tiles_per_batch, row_of):
    return lambda i, j: (row_of(i // tiles_per_batch), 0, 0)


def _proj_common_specs(d, tm, tpb, sh_row, sc_row):
    return [
        pl.BlockSpec((tm, d), lambda i, j: (i, 0)),
        pl.BlockSpec((1, d), lambda i, j: (0, 0)),
        pl.BlockSpec((1, 1, d), _mod_spec(tpb, sh_row)),
        pl.BlockSpec((1, 1, d), _mod_spec(tpb, sc_row)),
    ]


def _proj_gelu(x2, g, modt, sh_row, sc_row, w, seq, tm=1024, tn=1024):
    t, d = x2.shape
    n = w.shape[1]
    tpb = seq // tm
    return pl.pallas_call(
        _proj_gelu_kernel,
        grid=(t // tm, n // tn),
        in_specs=_proj_common_specs(d, tm, tpb, sh_row, sc_row) + [
            pl.BlockSpec((d, tn), lambda i, j: (0, j)),
        ],
        out_specs=pl.BlockSpec((tm, tn), lambda i, j: (i, j)),
        out_shape=jax.ShapeDtypeStruct((t, n), BF16),
        scratch_shapes=[pltpu.VMEM((tm, d), BF16)],
        compiler_params=_params(("arbitrary", "arbitrary")),
        name="proj_gelu",
    )(x2, g, modt, modt, w)


def _proj_heads(x2, g, modt, sh_row, sc_row, w, head_g, seq, *, norm, out_scale=1.0,
                transpose_chunk=0, tm=1024, tn=1024):
    t, d = x2.shape
    n = w.shape[1]
    nb = t // seq
    nh = n // HEAD_DIM
    tpb = seq // tm
    hpt = tn // HEAD_DIM
    kern = functools.partial(_proj_heads_kernel, norm=norm, out_scale=out_scale,
                             transpose_chunk=transpose_chunk)
    if transpose_chunk:
        cpt = tm // transpose_chunk
        out_spec = pl.BlockSpec((1, hpt, cpt, HEAD_DIM, transpose_chunk),
                                lambda i, j: (i // tpb, j, i % tpb, 0, 0))
        out_shape = (nb, nh, seq // transpose_chunk, HEAD_DIM, transpose_chunk)
    else:
        out_spec = pl.BlockSpec((1, hpt, tm, HEAD_DIM), lambda i, j: (i // tpb, j, i % tpb, 0))
        out_shape = (nb, nh, seq, HEAD_DIM)
    return pl.pallas_call(
        kern,
        grid=(t // tm, n // tn),
        in_specs=_proj_common_specs(d, tm, tpb, sh_row, sc_row) + [
            pl.BlockSpec((d, tn), lambda i, j: (0, j)),
            pl.BlockSpec((1, HEAD_DIM), lambda i, j: (0, 0)),
        ],
        out_specs=out_spec,
        out_shape=jax.ShapeDtypeStruct(out_shape, BF16),
        scratch_shapes=[pltpu.VMEM((tm, d), BF16)],
        compiler_params=_params(("arbitrary", "arbitrary")),
        name="proj_heads",
    )(x2, g, modt, modt, w, head_g)


def _proj_forget(x2, g, modt, sh_row, sc_row, w_pad, b_pad, n_heads, seq, tm=1024):
    t, d = x2.shape
    tpb = seq // tm
    kern = functools.partial(_proj_forget_kernel, n_heads=n_heads)
    return pl.pallas_call(
        kern,
        grid=(t // tm, 1),
        in_specs=_proj_common_specs(d, tm, tpb, sh_row, sc_row) + [
            pl.BlockSpec((d, LANES), lambda i, j: (0, 0)),
            pl.BlockSpec((1, LANES), lambda i, j: (0, 0)),
        ],
        out_specs=pl.BlockSpec((1, n_heads, tm), lambda i, j: (i // tpb, 0, i % tpb)),
        out_shape=jax.ShapeDtypeStruct((t // seq, n_heads, seq), F32),
        compiler_params=_params(("arbitrary", "arbitrary")),
        name="proj_forget",
    )(x2, g, modt, modt, w_pad, b_pad)


def _gate_kernel(u_ref, v_ref, lg_ref, lb_ref, ws_ref, bst_ref, o_ref, vn_scr):
    v = v_ref[...].astype(F32)
    mu = jnp.mean(v, axis=-1, keepdims=True)
    vc = v - mu
    var = jnp.mean(vc * vc, axis=-1, keepdims=True)
    vn_scr[...] = (vc * lax.rsqrt(var + EPS) * lg_ref[...] + lb_ref[...]).astype(BF16)

    row = lax.broadcasted_iota(jnp.int32, (GMLP_BLOCK, GMLP_BLOCK), 0) // CHUNK
    col = lax.broadcasted_iota(jnp.int32, (GMLP_BLOCK, GMLP_BLOCK), 1) // CHUNK
    keep = col <= row
    gdim = v.shape[1] // GMLP_GROUPS
    for gi in range(GMLP_GROUPS):
        w = jnp.where(keep, ws_ref[gi], 0.0).astype(BF16)
        bias = bst_ref[:, gi:gi + 1]
        cols = slice(gi * gdim, (gi + 1) * gdim)
        for nb in range(v.shape[0] // GMLP_BLOCK):
            rows = slice(nb * GMLP_BLOCK, (nb + 1) * GMLP_BLOCK)
            sv = jnp.dot(w, vn_scr[rows, cols], preferred_element_type=F32) + bias
            o_ref[rows, cols] = (u_ref[rows, cols].astype(F32) * sv).astype(o_ref.dtype)


def _gate(z, ln_g, ln_b, ws, bs_t, tm=256):
    t, n2 = z.shape
    half = n2 // 2
    return pl.pallas_call(
        _gate_kernel,
        grid=(t // tm,),
        in_specs=[
            pl.BlockSpec((tm, half), lambda i: (i, 0)),
            pl.BlockSpec((tm, half), lambda i: (i, 1)),
            pl.BlockSpec((1, half), lambda i: (0, 0)),
            pl.BlockSpec((1, half), lambda i: (0, 0)),
            pl.BlockSpec(ws.shape, lambda i: (0, 0, 0)),
            pl.BlockSpec(bs_t.shape, lambda i: (0, 0)),
        ],
        out_specs=pl.BlockSpec((tm, half), lambda i: (i, 0)),
        out_shape=jax.ShapeDtypeStruct((t, half), BF16),
        scratch_shapes=[pltpu.VMEM((tm, half), BF16)],
        compiler_params=_params(("arbitrary",)),
        name="gate",
    )(z, z, ln_g, ln_b, ws, bs_t)


def _accumulate_gated(o_ref, x_ref, gate_ref, d, k, nk):
    if nk == 1:
        o_ref[...] = x_ref[...] + gate_ref[0] * d
        return

    @pl.when(k == 0)
    def _():
        o_ref[...] = d

    @pl.when(jnp.logical_and(k > 0, k < nk - 1))
    def _():
        o_ref[...] += d

    @pl.when(k == nk - 1)
    def _():
        o_ref[...] = x_ref[...] + gate_ref[0] * (o_ref[...] + d)


def _outproj_kernel(a_ref, w_ref, x_ref, gate_ref, o_ref, *, nk):
    d = jnp.dot(a_ref[...], w_ref[...], preferred_element_type=F32)
    _accumulate_gated(o_ref, x_ref, gate_ref, d, pl.program_id(1), nk)


def _outproj(a, w, x2, modt, gate_row, seq, tm=512, tk=1024):
    t, kdim = a.shape
    d = w.shape[1]
    tpb = seq // tm
    nk = kdim // tk
    return pl.pallas_call(
        functools.partial(_outproj_kernel, nk=nk),
        grid=(t // tm, nk),
        in_specs=[
            pl.BlockSpec((tm, tk), lambda i, k: (i, k)),
            pl.BlockSpec((tk, d), lambda i, k: (k, 0)),
            pl.BlockSpec((tm, d), lambda i, k: (i, 0)),
            pl.BlockSpec((1, 1, d), _mod_spec(tpb, gate_row)),
        ],
        out_specs=pl.BlockSpec((tm, d), lambda i, k: (i, 0)),
        out_shape=jax.ShapeDtypeStruct((t, d), F32),
        compiler_params=_params(("arbitrary", "arbitrary")),
        name="outproj",
    )(a, w, x2, modt)


def _mlp_kernel(x_ref, g_ref, sh_ref, sc_ref, gate_ref, w1_ref, w2_ref, o_ref, h_scr, *, nj):
    _prologue(x_ref, g_ref, sh_ref, sc_ref, h_scr)
    a = jnp.dot(h_scr[...], w1_ref[...], preferred_element_type=F32)
    a = jnp.square(jnp.maximum(a, 0.0)).astype(BF16)
    d = jnp.dot(a, w2_ref[...], preferred_element_type=F32)
    _accumulate_gated(o_ref, x_ref, gate_ref, d, pl.program_id(1), nj)


def _mlp(x2, g, modt, sh_row, sc_row, gate_row, w1, w2, seq, tm=512, tf=1024):
    t, d = x2.shape
    dff = w1.shape[1]
    tpb = seq // tm
    nj = dff // tf
    return pl.pallas_call(
        functools.partial(_mlp_kernel, nj=nj),
        grid=(t // tm, nj),
        in_specs=_proj_common_specs(d, tm, tpb, sh_row, sc_row) + [
            pl.BlockSpec((1, 1, d), _mod_spec(tpb, gate_row)),
            pl.BlockSpec((d, tf), lambda i, j: (0, j)),
            pl.BlockSpec((tf, d), lambda i, j: (j, 0)),
        ],
        out_specs=pl.BlockSpec((tm, d), lambda i, j: (i, 0)),
        out_shape=jax.ShapeDtypeStruct((t, d), F32),
        scratch_shapes=[pltpu.VMEM((tm, d), BF16)],
        compiler_params=_params(("arbitrary", "arbitrary")),
        name="mlp",
    )(x2, g, modt, modt, modt, w1, w2)


def _cumsum_kernel(x_ref, o_ref):
    x = x_ref[...]
    n = x.shape[1]
    lane = lax.broadcasted_iota(jnp.int32, x.shape, 1)
    d = 1
    while d < n:
        x = x + jnp.where(lane >= d, pltpu.roll(x, d, axis=1), 0.0)
        d *= 2
    o_ref[...] = x


def _cumsum(x):
    return pl.pallas_call(
        _cumsum_kernel,
        out_shape=jax.ShapeDtypeStruct(x.shape, x.dtype),
        name="cumsum",
    )(x)


def _flash_kernel(q_ref, k_ref, vt_ref, f_ref, o_ref, frep_scr, sa_scr, sb_scr, m_scr, acc_scr,
                  *, tq):
    qi = pl.program_id(2)
    tk = tq
    n_lane_groups = tq // LANES

    @pl.when(qi == 0)
    def _():
        f = f_ref[0, 0]
        frep_scr[...] = jnp.broadcast_to(f, (LANES, f.shape[1])).T * LOG2E

    q = q_ref[0, 0]
    f_q0 = frep_scr[pl.ds(pl.multiple_of(qi * tq, tq), 1), :]
    ones = jnp.ones((2 * SUBLANES, tk), BF16)

    m_scr[...] = jnp.full(m_scr.shape, -jnp.inf, F32)
    acc_scr[...] = jnp.zeros(acc_scr.shape, F32)

    def causal(s):
        key = lax.broadcasted_iota(jnp.int32, s.shape, 0)
        qry = lax.broadcasted_iota(jnp.int32, s.shape, 1)
        return jnp.where(key <= qry, s, -jnp.inf)

    def scores(c, masked=False):
        rows = pl.ds(pl.multiple_of(c * tk, tk), tk)
        s = lax.dot_general(k_ref[0, 0, rows, :], q, (((1,), (1,)), ((), ())),
                            preferred_element_type=F32)
        bias = f_q0 - frep_scr[rows, :]
        s = s + jnp.concatenate([bias] * n_lane_groups, axis=1)
        return causal(s) if masked else s

    def update(c, s_ref):
        m_old = m_scr[...]
        m_new = jnp.maximum(m_old, jnp.max(s_ref[...], axis=0, keepdims=True))
        p = jnp.exp2(s_ref[...] - m_new).astype(BF16)
        alpha = jnp.exp2(m_old - m_new)
        v_aug = jnp.concatenate([vt_ref[0, 0, c], ones], axis=0)
        acc_scr[...] = alpha * acc_scr[...] + jnp.dot(v_aug, p, preferred_element_type=F32)
        m_scr[...] = m_new

    sa_scr[...] = scores(0)

    def pair(j, carry):
        c0 = 2 * j
        sb_scr[...] = scores(c0 + 1)
        update(c0, sa_scr)
        sa_scr[...] = scores(c0 + 2)
        update(c0 + 1, sb_scr)
        return carry

    lax.fori_loop(0, qi // 2, pair, 0)

    @pl.when(qi % 2 == 0)
    def _():
        sa_scr[...] = causal(sa_scr[...])
        update(qi, sa_scr)

    @pl.when(qi % 2 == 1)
    def _():
        sb_scr[...] = scores(qi, masked=True)
        update(qi - 1, sa_scr)
        update(qi, sb_scr)

    acc = acc_scr[...]
    inv_l = 1.0 / acc[HEAD_DIM:HEAD_DIM + 1]
    o_ref[0] = (acc[:HEAD_DIM] * inv_l).T.astype(o_ref.dtype)


def _flash(q, k, vt, fcum, tq):
    nb, nh, seq, dh = q.shape
    nq = seq // tq
    assert vt.shape == (nb, nh, nq, dh, tq)
    return pl.pallas_call(
        functools.partial(_flash_kernel, tq=tq),
        grid=(nb, nh, nq),
        in_specs=[
            pl.BlockSpec((1, 1, tq, dh), lambda b, h, i: (b, h, i, 0)),
            pl.BlockSpec((1, 1, seq, dh), lambda b, h, i: (b, h, 0, 0)),
            pl.BlockSpec((1, 1, nq, dh, tq), lambda b, h, i: (b, h, 0, 0, 0)),
            pl.BlockSpec((1, 1, 1, seq), lambda b, h, i: (b, h, 0, 0)),
        ],
        out_specs=pl.BlockSpec((1, tq, dh), lambda b, h, i: (b, i, h)),
        out_shape=jax.ShapeDtypeStruct((nb, seq, nh * dh), BF16),
        scratch_shapes=[
            pltpu.VMEM((seq, LANES), F32),
            pltpu.VMEM((tq, tq), F32),
            pltpu.VMEM((tq, tq), F32),
            pltpu.VMEM((1, tq), F32),
            pltpu.VMEM((dh + 2 * SUBLANES, tq), F32),
        ],
        compiler_params=_params(("arbitrary", "arbitrary", "arbitrary")),
        name="flash",
    )(q, k, vt, fcum.reshape(nb, nh, 1, seq))


def kernel(x, c, ada_w, ada_b, norm_g, mlp_w1, mlp_w2, gmlp_w_in, gmlp_ln_g, gmlp_ln_b,
           gmlp_ws, gmlp_bs, gmlp_w_out, kv_norm_g, kv_ada_w, kv_ada_b, w_kv, k_norm_g,
           w_f, b_f, attn_wq, q_norm_g, attn_wo):
    nb, seq, d = x.shape
    depth = ada_w.shape[0]
    n_a = gmlp_w_in.shape[0]
    n_heads = w_f.shape[1]
    half = gmlp_w_out.shape[1]

    modt = _mod(c, ada_w, ada_b).reshape(depth * nb * N_MOD, 1, d)
    kvmod = _mod(c, kv_ada_w[None], kv_ada_b[None]).reshape(nb * 2, 1, d)

    def mod_row(layer, which):
        return lambda b: (layer * nb + b) * N_MOD + which

    x2 = x.reshape(nb * seq, d)
    inv_sqrt = 1.0 / float(HEAD_DIM) ** 0.5
    k_heads = v_heads = fcum = None
    for layer in range(depth):
        g1 = norm_g[layer, 0][None]
        g2 = norm_g[layer, 1][None]
        if layer < n_a:
            a = layer
            z = _proj_gelu(x2, g1, modt, mod_row(layer, 0), mod_row(layer, 1),
                           gmlp_w_in[a].astype(BF16), seq)
            p = _gate(z, gmlp_ln_g[a][None], gmlp_ln_b[a][None], gmlp_ws[a], gmlp_bs[a].T)
            x2 = _outproj(p, gmlp_w_out[a].astype(BF16), x2, modt, mod_row(layer, 2), seq)
        else:
            if layer == n_a:
                kv_sh = lambda b: b * 2
                kv_sc = lambda b: b * 2 + 1
                kvg = kv_norm_g[None]
                k_heads = _proj_heads(x2, kvg, kvmod, kv_sh, kv_sc,
                                      w_kv[:, :d].astype(BF16), k_norm_g[None], seq, norm=True)
                v_heads = _proj_heads(x2, kvg, kvmod, kv_sh, kv_sc,
                                      w_kv[:, d:].astype(BF16), k_norm_g[None], seq, norm=False,
                                      transpose_chunk=FLASH_TILE)
                w_f_pad = jnp.pad(w_f, ((0, 0), (0, LANES - n_heads))).astype(BF16)
                b_f_pad = jnp.pad(b_f, (0, LANES - n_heads))[None]
                logf = _proj_forget(x2, kvg, kvmod, kv_sh, kv_sc, w_f_pad, b_f_pad,
                                    n_heads, seq)
                fcum = _cumsum(logf.reshape(nb * n_heads, seq)).reshape(nb, n_heads, seq)
            bl = layer - n_a
            q_heads = _proj_heads(x2, g1, modt, mod_row(layer, 0), mod_row(layer, 1),
                                  attn_wq[bl].astype(BF16), q_norm_g[bl][None], seq,
                                  norm=True, out_scale=inv_sqrt * LOG2E)
            o = _flash(q_heads, k_heads, v_heads, fcum, tq=FLASH_TILE)
            x2 = _outproj(o.reshape(nb * seq, d), attn_wo[bl].astype(BF16), x2, modt,
                          mod_row(layer, 2), seq)
        x2 = _mlp(x2, g2, modt, mod_row(layer, 3), mod_row(layer, 4), mod_row(layer, 5),
                  mlp_w1[layer].astype(BF16), mlp_w2[layer].astype(BF16), seq)
    return x2.reshape(nb, seq, d)
```

```python
import functools

import jax
import jax.numpy as jnp
from jax import lax
from jax.experimental import pallas as pl
from jax.experimental.pallas import tpu as pltpu

F32 = jnp.float32
BF16 = jnp.bfloat16

EPS = 1e-6
CHUNK = 64
GMLP_BLOCK = 128
GMLP_GROUPS = 8
HEAD_DIM = 128
N_MOD = 6
LANES = 128
SUBLANES = 8
MXU_DIM = 256
LOG2E = 1.4426950408889634
FLASH_TILE = 512

MIB = 1024 * 1024
VMEM_LIMIT = 56 * MIB


def _params(semantics, vmem=VMEM_LIMIT):
    return pltpu.CompilerParams(dimension_semantics=semantics, vmem_limit_bytes=vmem)


def _tile(n, target, align=MXU_DIM):
    best = None
    for t in range(align, min(n, target) + 1, align):
        if n % t == 0:
            best = t
    assert best is not None, (n, target, align)
    return best


def _adaln(x, g, shift, scale):
    ms = jnp.mean(x * x, axis=-1, keepdims=True)
    return (x * lax.rsqrt(ms + EPS)) * (g * (1.0 + scale)) + shift


def _gelu(a):
    return 0.5 * a * (1.0 + lax.erf(a * (0.5 ** 0.5)))


def _mod_kernel(c_ref, w_ref, b_ref, o_ref):
    sc = jax.nn.silu(c_ref[...])
    o_ref[0] = jnp.dot(sc, w_ref[0], preferred_element_type=F32) + b_ref[0]


def _mod(c, w, b, tn=1024):
    nl, d, n = w.shape
    nb = c.shape[0]
    tn = _tile(n, tn)
    return pl.pallas_call(
        _mod_kernel,
        grid=(nl, n // tn),
        in_specs=[
            pl.BlockSpec((nb, d), lambda l, j: (0, 0)),
            pl.BlockSpec((1, d, tn), lambda l, j: (l, 0, j)),
            pl.BlockSpec((1, 1, tn), lambda l, j: (l, 0, j)),
        ],
        out_specs=pl.BlockSpec((1, nb, tn), lambda l, j: (l, 0, j)),
        out_shape=jax.ShapeDtypeStruct((nl, nb, n), F32),
        compiler_params=_params(("arbitrary", "arbitrary")),
        name="mod",
    )(c, w, b.reshape(nl, 1, n))


def _ahead(x_ref, g_ref, sh_ref, sc_ref, h_a, h_b, nj, main):
    i = pl.program_id(0)
    j = pl.program_id(1)
    rows_per_step = x_ref.shape[0] // nj

    def fill(h_dst):
        rows = pl.ds(pl.multiple_of(j * rows_per_step, rows_per_step), rows_per_step)
        h = _adaln(x_ref[rows, :], g_ref[...], sh_ref[0], sc_ref[0])
        h_dst[rows, :] = h.astype(BF16)

    @pl.when(i == 0)
    def _():
        fill(h_a)

    @pl.when(jnp.logical_and(i > 0, i % 2 == 0))
    def _():
        fill(h_a)
        main(h_b)

    @pl.when(i % 2 == 1)
    def _():
        fill(h_b)
        main(h_a)


def _ahead_specs(d, tm, n_tiles, tpb, sh_row, sc_row):
    fill_tile = lambda i: jnp.minimum(i, n_tiles - 1)
    return [
        pl.BlockSpec((tm, d), lambda i, j: (fill_tile(i), 0)),
        pl.BlockSpec((1, d), lambda i, j: (0, 0)),
        pl.BlockSpec((1, 1, d), lambda i, j: (sh_row(fill_tile(i) // tpb), 0, 0)),
        pl.BlockSpec((1, 1, d), lambda i, j: (sc_row(fill_tile(i) // tpb), 0, 0)),
    ]


def _out_tile(i):
    return jnp.maximum(i - 1, 0)


def _col_step(i, j):
    return jnp.where(i == 0, 0, j)


def _proj_gelu_kernel(x_ref, g_ref, sh_ref, sc_ref, w_ref, o_ref, h_a, h_b, *, nj):
    def main(h):
        a = jnp.dot(h[...], w_ref[...], preferred_element_type=F32)
        o_ref[...] = _gelu(a).astype(o_ref.dtype)

    _ahead(x_ref, g_ref, sh_ref, sc_ref, h_a, h_b, nj, main)


def _proj_gelu(x2, g, modt, sh_row, sc_row, w, seq, tm=1024, tn=1536):
    t, d = x2.shape
    n = w.shape[1]
    tm = _tile(seq, tm)
    tn = _tile(n, tn)
    tpb = seq // tm
    n_tiles = t // tm
    nj = n // tn
    return pl.pallas_call(
        functools.partial(_proj_gelu_kernel, nj=nj),
        grid=(n_tiles + 1, nj),
        in_specs=_ahead_specs(d, tm, n_tiles, tpb, sh_row, sc_row) + [
            pl.BlockSpec((d, tn), lambda i, j: (0, _col_step(i, j))),
        ],
        out_specs=pl.BlockSpec((tm, tn), lambda i, j: (_out_tile(i), _col_step(i, j))),
        out_shape=jax.ShapeDtypeStruct((t, n), BF16),
        scratch_shapes=[pltpu.VMEM((tm, d), BF16), pltpu.VMEM((tm, d), BF16)],
        compiler_params=_params(("arbitrary", "arbitrary")),
        name="proj_gelu",
    )(x2, g, modt, modt, w)


def _proj_heads_kernel(x_ref, g_ref, sh_ref, sc_ref, w_ref, hg_ref, o_ref, h_a, h_b, *,
                       nj, norm, out_scale, transpose_chunk):
    def main(h):
        a = jnp.dot(h[...], w_ref[...], preferred_element_type=F32)
        for hh in range(a.shape[1] // HEAD_DIM):
            t = a[:, hh * HEAD_DIM:(hh + 1) * HEAD_DIM]
            if norm:
                ms = jnp.mean(t * t, axis=-1, keepdims=True)
                t = t * lax.rsqrt(ms + EPS) * hg_ref[...]
            if out_scale != 1.0:
                t = t * out_scale
            if transpose_chunk:
                tt = t.T
                for cc in range(t.shape[0] // transpose_chunk):
                    cols = slice(cc * transpose_chunk, (cc + 1) * transpose_chunk)
                    o_ref[0, hh, cc] = tt[:, cols].astype(o_ref.dtype)
            else:
                o_ref[0, hh] = t.astype(o_ref.dtype)

    _ahead(x_ref, g_ref, sh_ref, sc_ref, h_a, h_b, nj, main)


def _proj_heads(x2, g, modt, sh_row, sc_row, w, head_g, seq, *, norm, out_scale=1.0,
                transpose_chunk=0, tm=512, tn=2048):
    t, d = x2.shape
    n = w.shape[1]
    tm = _tile(seq, tm)
    tn = _tile(n, tn)
    nb = t // seq
    nh = n // HEAD_DIM
    tpb = seq // tm
    n_tiles = t // tm
    nj = n // tn
    hpt = tn // HEAD_DIM
    kern = functools.partial(_proj_heads_kernel, nj=nj, norm=norm, out_scale=out_scale,
                             transpose_chunk=transpose_chunk)
    if transpose_chunk:
        cpt = tm // transpose_chunk
        out_spec = pl.BlockSpec(
            (1, hpt, cpt, HEAD_DIM, transpose_chunk),
            lambda i, j: (_out_tile(i) // tpb, _col_step(i, j), _out_tile(i) % tpb, 0, 0))
        out_shape = (nb, nh, seq // transpose_chunk, HEAD_DIM, transpose_chunk)
    else:
        out_spec = pl.BlockSpec(
            (1, hpt, tm, HEAD_DIM),
            lambda i, j: (_out_tile(i) // tpb, _col_step(i, j), _out_tile(i) % tpb, 0))
        out_shape = (nb, nh, seq, HEAD_DIM)
    return pl.pallas_call(
        kern,
        grid=(n_tiles + 1, nj),
        in_specs=_ahead_specs(d, tm, n_tiles, tpb, sh_row, sc_row) + [
            pl.BlockSpec((d, tn), lambda i, j: (0, _col_step(i, j))),
            pl.BlockSpec((1, HEAD_DIM), lambda i, j: (0, 0)),
        ],
        out_specs=out_spec,
        out_shape=jax.ShapeDtypeStruct(out_shape, BF16),
        scratch_shapes=[pltpu.VMEM((tm, d), BF16), pltpu.VMEM((tm, d), BF16)],
        compiler_params=_params(("arbitrary", "arbitrary")),
        name="proj_heads",
    )(x2, g, modt, modt, w, head_g)


def _proj_forget_kernel(x_ref, g_ref, sh_ref, sc_ref, w_ref, b_ref, o_ref, *, n_heads):
    h = _adaln(x_ref[...], g_ref[...], sh_ref[0], sc_ref[0]).astype(BF16)
    f = jnp.dot(h, w_ref[...], preferred_element_type=F32) + b_ref[...]
    o_ref[0] = jax.nn.log_sigmoid(f).T[:n_heads]


def _proj_forget(x2, g, modt, sh_row, sc_row, w_pad, b_pad, n_heads, seq, tm=1024):
    t, d = x2.shape
    tm = _tile(seq, tm)
    tpb = seq // tm
    kern = functools.partial(_proj_forget_kernel, n_heads=n_heads)
    return pl.pallas_call(
        kern,
        grid=(t // tm,),
        in_specs=[
            pl.BlockSpec((tm, d), lambda i: (i, 0)),
            pl.BlockSpec((1, d), lambda i: (0, 0)),
            pl.BlockSpec((1, 1, d), lambda i: (sh_row(i // tpb), 0, 0)),
            pl.BlockSpec((1, 1, d), lambda i: (sc_row(i // tpb), 0, 0)),
            pl.BlockSpec((d, LANES), lambda i: (0, 0)),
            pl.BlockSpec((1, LANES), lambda i: (0, 0)),
        ],
        out_specs=pl.BlockSpec((1, n_heads, tm), lambda i: (i // tpb, 0, i % tpb)),
        out_shape=jax.ShapeDtypeStruct((t // seq, n_heads, seq), F32),
        compiler_params=_params(("arbitrary",)),
        name="proj_forget",
    )(x2, g, modt, modt, w_pad, b_pad)


def _gate_kernel(u_ref, v_ref, lg_ref, lb_ref, ws_ref, bst_ref, o_ref, vn_scr):
    v = v_ref[...].astype(F32)
    mu = jnp.mean(v, axis=-1, keepdims=True)
    vc = v - mu
    var = jnp.mean(vc * vc, axis=-1, keepdims=True)
    vn_scr[...] = (vc * lax.rsqrt(var + EPS) * lg_ref[...] + lb_ref[...]).astype(BF16)

    row = lax.broadcasted_iota(jnp.int32, (GMLP_BLOCK, GMLP_BLOCK), 0) // CHUNK
    col = lax.broadcasted_iota(jnp.int32, (GMLP_BLOCK, GMLP_BLOCK), 1) // CHUNK
    keep = col <= row
    gdim = v.shape[1] // GMLP_GROUPS
    for gi in range(GMLP_GROUPS):
        w = jnp.where(keep, ws_ref[gi], 0.0).astype(BF16)
        bias = bst_ref[:, gi:gi + 1]
        cols = slice(gi * gdim, (gi + 1) * gdim)
        for nb in range(v.shape[0] // GMLP_BLOCK):
            rows = slice(nb * GMLP_BLOCK, (nb + 1) * GMLP_BLOCK)
            sv = jnp.dot(w, vn_scr[rows, cols], preferred_element_type=F32) + bias
            o_ref[rows, cols] = (u_ref[rows, cols].astype(F32) * sv).astype(o_ref.dtype)


def _gate(z, ln_g, ln_b, ws, bs_t, tm=256):
    t, n2 = z.shape
    half = n2 // 2
    return pl.pallas_call(
        _gate_kernel,
        grid=(t // tm,),
        in_specs=[
            pl.BlockSpec((tm, half), lambda i: (i, 0)),
            pl.BlockSpec((tm, half), lambda i: (i, 1)),
            pl.BlockSpec((1, half), lambda i: (0, 0)),
            pl.BlockSpec((1, half), lambda i: (0, 0)),
            pl.BlockSpec(ws.shape, lambda i: (0, 0, 0)),
            pl.BlockSpec(bs_t.shape, lambda i: (0, 0)),
        ],
        out_specs=pl.BlockSpec((tm, half), lambda i: (i, 0)),
        out_shape=jax.ShapeDtypeStruct((t, half), BF16),
        scratch_shapes=[pltpu.VMEM((tm, half), BF16)],
        compiler_params=_params(("arbitrary",)),
        name="gate",
    )(z, z, ln_g, ln_b, ws, bs_t)


def _outproj_kernel(a_ref, w_ref, x_ref, gate_ref, o_ref, *, nk):
    if nk == 1:
        d = jnp.dot(a_ref[...], w_ref[...], preferred_element_type=F32)
        o_ref[...] = x_ref[...] + gate_ref[0] * d
        return

    @pl.when(pl.program_id(1) == 0)
    def _():
        o_ref[...] = x_ref[...]

    d = jnp.dot(a_ref[...], w_ref[...], preferred_element_type=F32)
    o_ref[...] += gate_ref[0] * d


def _outproj(a, w, x2, modt, gate_row, seq, tm=512, tk=2048):
    t, kdim = a.shape
    d = w.shape[1]
    tm = _tile(seq, tm)
    tk = _tile(kdim, tk)
    tpb = seq // tm
    nk = kdim // tk
    return pl.pallas_call(
        functools.partial(_outproj_kernel, nk=nk),
        grid=(t // tm, nk),
        in_specs=[
            pl.BlockSpec((tm, tk), lambda i, k: (i, k)),
            pl.BlockSpec((tk, d), lambda i, k: (k, 0)),
            pl.BlockSpec((tm, d), lambda i, k: (i, 0)),
            pl.BlockSpec((1, 1, d), lambda i, k: (gate_row(i // tpb), 0, 0)),
        ],
        out_specs=pl.BlockSpec((tm, d), lambda i, k: (i, 0)),
        out_shape=jax.ShapeDtypeStruct((t, d), F32),
        compiler_params=_params(("arbitrary", "arbitrary")),
        name="outproj",
    )(a, w, x2, modt)


def _mlp_kernel(x_ref, g_ref, sh_ref, sc_ref, xres_ref, gate_ref, w1_ref, w2_ref, o_ref,
                h_a, h_b, *, nj):
    @pl.when(pl.program_id(1) == 0)
    def _():
        o_ref[...] = xres_ref[...]

    def main(h):
        a = jnp.dot(h[...], w1_ref[...], preferred_element_type=F32)
        a = jnp.square(jnp.maximum(a, 0.0)).astype(BF16)
        o_ref[...] += gate_ref[0] * jnp.dot(a, w2_ref[...], preferred_element_type=F32)

    _ahead(x_ref, g_ref, sh_ref, sc_ref, h_a, h_b, nj, main)


def _mlp(x2, g, modt, sh_row, sc_row, gate_row, w1, w2, seq, tm=512, tf=1024):
    t, d = x2.shape
    dff = w1.shape[1]
    tm = _tile(seq, tm)
    tf = _tile(dff, tf)
    tpb = seq // tm
    n_tiles = t // tm
    nj = dff // tf
    return pl.pallas_call(
        functools.partial(_mlp_kernel, nj=nj),
        grid=(n_tiles + 1, nj),
        in_specs=_ahead_specs(d, tm, n_tiles, tpb, sh_row, sc_row) + [
            pl.BlockSpec((tm, d), lambda i, j: (_out_tile(i), 0)),
            pl.BlockSpec((1, 1, d), lambda i, j: (gate_row(_out_tile(i) // tpb), 0, 0)),
            pl.BlockSpec((d, tf), lambda i, j: (0, _col_step(i, j))),
            pl.BlockSpec((tf, d), lambda i, j: (_col_step(i, j), 0)),
        ],
        out_specs=pl.BlockSpec((tm, d), lambda i, j: (_out_tile(i), 0)),
        out_shape=jax.ShapeDtypeStruct((t, d), F32),
        scratch_shapes=[pltpu.VMEM((tm, d), BF16), pltpu.VMEM((tm, d), BF16)],
        compiler_params=_params(("arbitrary", "arbitrary")),
        name="mlp",
    )(x2, g, modt, modt, x2, modt, w1, w2)


def _cumsum_kernel(x_ref, o_ref):
    x = x_ref[...]
    n = x.shape[1]
    lane = lax.broadcasted_iota(jnp.int32, x.shape, 1)
    d = 1
    while d < n:
        x = x + jnp.where(lane >= d, pltpu.roll(x, d, axis=1), 0.0)
        d *= 2
    o_ref[...] = x


def _cumsum(x):
    return pl.pallas_call(
        _cumsum_kernel,
        out_shape=jax.ShapeDtypeStruct(x.shape, x.dtype),
        name="cumsum",
    )(x)


def _flash_kernel(q_ref, k_ref, vt_ref, f_ref, o_ref, frep_scr, sa_scr, sb_scr, m_scr, acc_scr,
                  *, tq):
    qi = pl.program_id(2)
    tk = tq
    n_lane_groups = tq // LANES

    @pl.when(qi == 0)
    def _():
        f = f_ref[0, 0]
        frep_scr[...] = jnp.broadcast_to(f, (LANES, f.shape[1])).T * LOG2E

    q = q_ref[0, 0]
    f_q0 = frep_scr[pl.ds(pl.multiple_of(qi * tq, tq), 1), :]
    ones = jnp.ones((2 * SUBLANES, tk), BF16)

    m_scr[...] = jnp.full(m_scr.shape, -jnp.inf, F32)
    acc_scr[...] = jnp.zeros(acc_scr.shape, F32)

    def causal(s):
        key = lax.broadcasted_iota(jnp.int32, s.shape, 0)
        qry = lax.broadcasted_iota(jnp.int32, s.shape, 1)
        return jnp.where(key <= qry, s, -jnp.inf)

    def scores(c, masked=False):
        rows = pl.ds(pl.multiple_of(c * tk, tk), tk)
        s = lax.dot_general(k_ref[0, 0, rows, :], q, (((1,), (1,)), ((), ())),
                            preferred_element_type=F32)
        bias = f_q0 - frep_scr[rows, :]
        s = s + jnp.concatenate([bias] * n_lane_groups, axis=1)
        return causal(s) if masked else s

    def update(c, s_ref):
        m_old = m_scr[...]
        m_new = jnp.maximum(m_old, jnp.max(s_ref[...], axis=0, keepdims=True))
        p = jnp.exp2(s_ref[...] - m_new).astype(BF16)
        alpha = jnp.exp2(m_old - m_new)
        v_aug = jnp.concatenate([vt_ref[0, 0, c], ones], axis=0)
        acc_scr[...] = alpha * acc_scr[...] + jnp.dot(v_aug, p, preferred_element_type=F32)
        m_scr[...] = m_new

    sa_scr[...] = scores(0)

    def pair(j, carry):
        c0 = 2 * j
        sb_scr[...] = scores(c0 + 1)
        update(c0, sa_scr)
        sa_scr[...] = scores(c0 + 2)
        update(c0 + 1, sb_scr)
        return carry

    lax.fori_loop(0, qi // 2, pair, 0)

    @pl.when(qi % 2 == 0)
    def _():
        sa_scr[...] = causal(sa_scr[...])
        update(qi, sa_scr)

    @pl.when(qi % 2 == 1)
    def _():
        sb_scr[...] = scores(qi, masked=True)
        update(qi - 1, sa_scr)
        update(qi, sb_scr)

    acc = acc_scr[...]
    inv_l = 1.0 / acc[HEAD_DIM:HEAD_DIM + 1]
    o_ref[0] = (acc[:HEAD_DIM] * inv_l).T.astype(o_ref.dtype)


def _flash(q, k, vt, fcum, tq):
    nb, nh, seq, dh = q.shape
    nq = seq // tq
    assert vt.shape == (nb, nh, nq, dh, tq)
    return pl.pallas_call(
        functools.partial(_flash_kernel, tq=tq),
        grid=(nb, nh, nq),
        in_specs=[
            pl.BlockSpec((1, 1, tq, dh), lambda b, h, i: (b, h, i, 0)),
            pl.BlockSpec((1, 1, seq, dh), lambda b, h, i: (b, h, 0, 0)),
            pl.BlockSpec((1, 1, nq, dh, tq), lambda b, h, i: (b, h, 0, 0, 0)),
            pl.BlockSpec((1, 1, 1, seq), lambda b, h, i: (b, h, 0, 0)),
        ],
        out_specs=pl.BlockSpec((1, tq, dh), lambda b, h, i: (b, i, h)),
        out_shape=jax.ShapeDtypeStruct((nb, seq, nh * dh), BF16),
        scratch_shapes=[
            pltpu.VMEM((seq, LANES), F32),
            pltpu.VMEM((tq, tq), F32),
            pltpu.VMEM((tq, tq), F32),
            pltpu.VMEM((1, tq), F32),
            pltpu.VMEM((dh + 2 * SUBLANES, tq), F32),
        ],
        compiler_params=_params(("arbitrary", "arbitrary", "arbitrary")),
        name="flash",
    )(q, k, vt, fcum.reshape(nb, nh, 1, seq))


def kernel(x, c, ada_w, ada_b, norm_g, mlp_w1, mlp_w2, gmlp_w_in, gmlp_ln_g, gmlp_ln_b,
           gmlp_ws, gmlp_bs, gmlp_w_out, kv_norm_g, kv_ada_w, kv_ada_b, w_kv, k_norm_g,
           w_f, b_f, attn_wq, q_norm_g, attn_wo):
    nb, seq, d = x.shape
    depth = ada_w.shape[0]
    n_a = gmlp_w_in.shape[0]
    n_heads = w_f.shape[1]

    modt = _mod(c, ada_w, ada_b).reshape(depth * nb * N_MOD, 1, d)
    kvmod = _mod(c, kv_ada_w[None], kv_ada_b[None]).reshape(nb * 2, 1, d)

    def mod_row(layer, which):
        return lambda b: (layer * nb + b) * N_MOD + which

    x2 = x.reshape(nb * seq, d)
    inv_sqrt = 1.0 / float(HEAD_DIM) ** 0.5
    k_heads = v_heads = fcum = None
    for layer in range(depth):
        g1 = norm_g[layer, 0][None]
        g2 = norm_g[layer, 1][None]
        if layer < n_a:
            a = layer
            z = _proj_gelu(x2, g1, modt, mod_row(layer, 0), mod_row(layer, 1),
                           gmlp_w_in[a].astype(BF16), seq)
            p = _gate(z, gmlp_ln_g[a][None], gmlp_ln_b[a][None], gmlp_ws[a], gmlp_bs[a].T)
            x2 = _outproj(p, gmlp_w_out[a].astype(BF16), x2, modt, mod_row(layer, 2), seq)
        else:
            if layer == n_a:
                kv_sh = lambda b: b * 2
                kv_sc = lambda b: b * 2 + 1
                kvg = kv_norm_g[None]
                k_heads = _proj_heads(x2, kvg, kvmod, kv_sh, kv_sc,
                                      w_kv[:, :d].astype(BF16), k_norm_g[None], seq, norm=True)
                v_heads = _proj_heads(x2, kvg, kvmod, kv_sh, kv_sc,
                                      w_kv[:, d:].astype(BF16), k_norm_g[None], seq, norm=False,
                                      transpose_chunk=FLASH_TILE)
                w_f_pad = jnp.pad(w_f, ((0, 0), (0, LANES - n_heads))).astype(BF16)
                b_f_pad = jnp.pad(b_f, (0, LANES - n_heads))[None]
                logf = _proj_forget(x2, kvg, kvmod, kv_sh, kv_sc, w_f_pad, b_f_pad,
                                    n_heads, seq)
                fcum = _cumsum(logf.reshape(nb * n_heads, seq)).reshape(nb, n_heads, seq)
            bl = layer - n_a
            q_heads = _proj_heads(x2, g1, modt, mod_row(layer, 0), mod_row(layer, 1),
                                  attn_wq[bl].astype(BF16), q_norm_g[bl][None], seq,
                                  norm=True, out_scale=inv_sqrt * LOG2E)
            o = _flash(q_heads, k_heads, v_heads, fcum, tq=FLASH_TILE)
            x2 = _outproj(o.reshape(nb * seq, d), attn_wo[bl].astype(BF16), x2, modt,
                          mod_row(layer, 2), seq)
        x2 = _mlp(x2, g2, modt, mod_row(layer, 3), mod_row(layer, 4), mod_row(layer, 5),
                  mlp_w1[layer].astype(BF16), mlp_w2[layer].astype(BF16), seq)
    return x2.reshape(nb, seq, d)
```

```python
import functools

import jax
import jax.numpy as jnp
from jax import lax
from jax.experimental import pallas as pl
from jax.experimental.pallas import tpu as pltpu

F32 = jnp.float32
BF16 = jnp.bfloat16

EPS = 1e-6
CHUNK = 64
GMLP_BLOCK = 128
GMLP_GROUPS = 8
HEAD_DIM = 128
N_MOD = 6
LANES = 128
SUBLANES = 8
MXU_DIM = 256
LOG2E = 1.4426950408889634
FLASH_TILE = 512

MIB = 1024 * 1024
VMEM_LIMIT = 56 * MIB


def _params(semantics, vmem=VMEM_LIMIT):
    return pltpu.CompilerParams(dimension_semantics=semantics, vmem_limit_bytes=vmem)


def _tile(n, target, align=MXU_DIM):
    best = None
    for t in range(align, min(n, target) + 1, align):
        if n % t == 0:
            best = t
    assert best is not None, (n, target, align)
    return best


def _adaln(x, g, shift, scale):
    ms = jnp.mean(x * x, axis=-1, keepdims=True)
    return (x * lax.rsqrt(ms + EPS)) * (g * (1.0 + scale)) + shift


def _gelu(a):
    return 0.5 * a * (1.0 + lax.erf(a * (0.5 ** 0.5)))


def _mod_kernel(c_ref, w_ref, b_ref, o_ref):
    sc = jax.nn.silu(c_ref[...])
    o_ref[0] = jnp.dot(sc, w_ref[0], preferred_element_type=F32) + b_ref[0]


def _mod(c, w, b, tn=1024):
    nl, d, n = w.shape
    nb = c.shape[0]
    tn = _tile(n, tn)
    return pl.pallas_call(
        _mod_kernel,
        grid=(nl, n // tn),
        in_specs=[
            pl.BlockSpec((nb, d), lambda l, j: (0, 0)),
            pl.BlockSpec((1, d, tn), lambda l, j: (l, 0, j)),
            pl.BlockSpec((1, 1, tn), lambda l, j: (l, 0, j)),
        ],
        out_specs=pl.BlockSpec((1, nb, tn), lambda l, j: (l, 0, j)),
        out_shape=jax.ShapeDtypeStruct((nl, nb, n), F32),
        compiler_params=_params(("arbitrary", "arbitrary")),
        name="mod",
    )(c, w, b.reshape(nl, 1, n))


def _ahead(x_ref, g_ref, sh_ref, sc_ref, h_a, h_b, nj, main):
    i = pl.program_id(0)
    j = pl.program_id(1)
    rows_per_step = x_ref.shape[0] // nj

    def fill(h_dst):
        rows = pl.ds(pl.multiple_of(j * rows_per_step, rows_per_step), rows_per_step)
        h = _adaln(x_ref[rows, :], g_ref[...], sh_ref[0], sc_ref[0])
        h_dst[rows, :] = h.astype(BF16)

    @pl.when(i == 0)
    def _():
        fill(h_a)

    @pl.when(jnp.logical_and(i > 0, i % 2 == 0))
    def _():
        fill(h_a)
        main(h_b)

    @pl.when(i % 2 == 1)
    def _():
        fill(h_b)
        main(h_a)


def _ahead_specs(d, tm, n_tiles, tpb, sh_row, sc_row):
    fill_tile = lambda i: jnp.minimum(i, n_tiles - 1)
    return [
        pl.BlockSpec((tm, d), lambda i, j: (fill_tile(i), 0)),
        pl.BlockSpec((1, d), lambda i, j: (0, 0)),
        pl.BlockSpec((1, 1, d), lambda i, j: (sh_row(fill_tile(i) // tpb), 0, 0)),
        pl.BlockSpec((1, 1, d), lambda i, j: (sc_row(fill_tile(i) // tpb), 0, 0)),
    ]


def _out_tile(i):
    return jnp.maximum(i - 1, 0)


def _col_step(i, j):
    return jnp.where(i == 0, 0, j)


def _proj_gelu_kernel(x_ref, g_ref, sh_ref, sc_ref, w_ref, o_ref, h_a, h_b, *, nj):
    def main(h):
        a = jnp.dot(h[...], w_ref[...], preferred_element_type=F32)
        o_ref[...] = _gelu(a).astype(o_ref.dtype)

    _ahead(x_ref, g_ref, sh_ref, sc_ref, h_a, h_b, nj, main)


def _proj_gelu(x2, g, modt, sh_row, sc_row, w, seq, tm=1024, tn=1536):
    t, d = x2.shape
    n = w.shape[1]
    tm = _tile(seq, tm)
    tn = _tile(n, tn)
    tpb = seq // tm
    n_tiles = t // tm
    nj = n // tn
    return pl.pallas_call(
        functools.partial(_proj_gelu_kernel, nj=nj),
        grid=(n_tiles + 1, nj),
        in_specs=_ahead_specs(d, tm, n_tiles, tpb, sh_row, sc_row) + [
            pl.BlockSpec((d, tn), lambda i, j: (0, _col_step(i, j))),
        ],
        out_specs=pl.BlockSpec((tm, tn), lambda i, j: (_out_tile(i), _col_step(i, j))),
        out_shape=jax.ShapeDtypeStruct((t, n), BF16),
        scratch_shapes=[pltpu.VMEM((tm, d), BF16), pltpu.VMEM((tm, d), BF16)],
        compiler_params=_params(("arbitrary", "arbitrary")),
        name="proj_gelu",
    )(x2, g, modt, modt, w)


def _proj_heads_kernel(x_ref, g_ref, sh_ref, sc_ref, w_ref, hg_ref, o_ref, h_a, h_b, *,
                       nj, norm, out_scale, transpose_chunk):
    def main(h):
        a = jnp.dot(h[...], w_ref[...], preferred_element_type=F32)
        for hh in range(a.shape[1] // HEAD_DIM):
            t = a[:, hh * HEAD_DIM:(hh + 1) * HEAD_DIM]
            if norm:
                ms = jnp.mean(t * t, axis=-1, keepdims=True)
                t = t * lax.rsqrt(ms + EPS) * hg_ref[...]
            if out_scale != 1.0:
                t = t * out_scale
            if transpose_chunk:
                tt = t.T
                for cc in range(t.shape[0] // transpose_chunk):
                    cols = slice(cc * transpose_chunk, (cc + 1) * transpose_chunk)
                    o_ref[0, hh, cc] = tt[:, cols].astype(o_ref.dtype)
            else:
                o_ref[0, hh] = t.astype(o_ref.dtype)

    _ahead(x_ref, g_ref, sh_ref, sc_ref, h_a, h_b, nj, main)


def _proj_heads(x2, g, modt, sh_row, sc_row, w, head_g, seq, *, norm, out_scale=1.0,
                transpose_chunk=0, tm=512, tn=2048):
    t, d = x2.shape
    n = w.shape[1]
    tm = _tile(seq, tm)
    tn = _tile(n, tn)
    nb = t // seq
    nh = n // HEAD_DIM
    tpb = seq // tm
    n_tiles = t // tm
    nj = n // tn
    hpt = tn // HEAD_DIM
    kern = functools.partial(_proj_heads_kernel, nj=nj, norm=norm, out_scale=out_scale,
                             transpose_chunk=transpose_chunk)
    if transpose_chunk:
        cpt = tm // transpose_chunk
        out_spec = pl.BlockSpec(
            (1, hpt, cpt, HEAD_DIM, transpose_chunk),
            lambda i, j: (_out_tile(i) // tpb, _col_step(i, j), _out_tile(i) % tpb, 0, 0))
        out_shape = (nb, nh, seq // transpose_chunk, HEAD_DIM, transpose_chunk)
    else:
        out_spec = pl.BlockSpec(
            (1, hpt, tm, HEAD_DIM),
            lambda i, j: (_out_tile(i) // tpb, _col_step(i, j), _out_tile(i) % tpb, 0))
        out_shape = (nb, nh, seq, HEAD_DIM)
    return pl.pallas_call(
        kern,
        grid=(n_tiles + 1, nj),
        in_specs=_ahead_specs(d, tm, n_tiles, tpb, sh_row, sc_row) + [
            pl.BlockSpec((d, tn), lambda i, j: (0, _col_step(i, j))),
            pl.BlockSpec((1, HEAD_DIM), lambda i, j: (0, 0)),
        ],
        out_specs=out_spec,
        out_shape=jax.ShapeDtypeStruct(out_shape, BF16),
        scratch_shapes=[pltpu.VMEM((tm, d), BF16), pltpu.VMEM((tm, d), BF16)],
        compiler_params=_params(("arbitrary", "arbitrary")),
        name="proj_heads",
    )(x2, g, modt, modt, w, head_g)


def _proj_forget_kernel(x_ref, g_ref, sh_ref, sc_ref, w_ref, b_ref, o_ref, *, n_heads):
    h = _adaln(x_ref[...], g_ref[...], sh_ref[0], sc_ref[0]).astype(BF16)
    f = jnp.dot(h, w_ref[...], preferred_element_type=F32) + b_ref[...]
    o_ref[0] = jax.nn.log_sigmoid(f).T[:n_heads]


def _proj_forget(x2, g, modt, sh_row, sc_row, w_pad, b_pad, n_heads, seq, tm=1024):
    t, d = x2.shape
    tm = _tile(seq, tm)
    tpb = seq // tm
    kern = functools.partial(_proj_forget_kernel, n_heads=n_heads)
    return pl.pallas_call(
        kern,
        grid=(t // tm,),
        in_specs=[
            pl.BlockSpec((tm, d), lambda i: (i, 0)),
            pl.BlockSpec((1, d), lambda i: (0, 0)),
            pl.BlockSpec((1, 1, d), lambda i: (sh_row(i // tpb), 0, 0)),
            pl.BlockSpec((1, 1, d), lambda i: (sc_row(i // tpb), 0, 0)),
            pl.BlockSpec((d, LANES), lambda i: (0, 0)),
            pl.BlockSpec((1, LANES), lambda i: (0, 0)),
        ],
        out_specs=pl.BlockSpec((1, n_heads, tm), lambda i: (i // tpb, 0, i % tpb)),
        out_shape=jax.ShapeDtypeStruct((t // seq, n_heads, seq), F32),
        compiler_params=_params(("arbitrary",)),
        name="proj_forget",
    )(x2, g, modt, modt, w_pad, b_pad)


def _gate_kernel(u_ref, v_ref, lg_ref, lb_ref, ws_ref, bst_ref, o_ref, vn_scr):
    v = v_ref[...].astype(F32)
    mu = jnp.mean(v, axis=-1, keepdims=True)
    vc = v - mu
    var = jnp.mean(vc * vc, axis=-1, keepdims=True)
    vn_scr[...] = (vc * lax.rsqrt(var + EPS) * lg_ref[...] + lb_ref[...]).astype(BF16)

    row = lax.broadcasted_iota(jnp.int32, (GMLP_BLOCK, GMLP_BLOCK), 0) // CHUNK
    col = lax.broadcasted_iota(jnp.int32, (GMLP_BLOCK, GMLP_BLOCK), 1) // CHUNK
    keep = col <= row
    gdim = v.shape[1] // GMLP_GROUPS
    for gi in range(GMLP_GROUPS):
        w = jnp.where(keep, ws_ref[gi], 0.0).astype(BF16)
        bias = bst_ref[:, gi:gi + 1]
        cols = slice(gi * gdim, (gi + 1) * gdim)
        for nb in range(v.shape[0] // GMLP_BLOCK):
            rows = slice(nb * GMLP_BLOCK, (nb + 1) * GMLP_BLOCK)
            sv = jnp.dot(w, vn_scr[rows, cols], preferred_element_type=F32) + bias
            o_ref[rows, cols] = (u_ref[rows, cols].astype(F32) * sv).astype(o_ref.dtype)


def _gate(z, ln_g, ln_b, ws, bs_t, tm=256):
    t, n2 = z.shape
    half = n2 // 2
    return pl.pallas_call(
        _gate_kernel,
        grid=(t // tm,),
        in_specs=[
            pl.BlockSpec((tm, half), lambda i: (i, 0)),
            pl.BlockSpec((tm, half), lambda i: (i, 1)),
            pl.BlockSpec((1, half), lambda i: (0, 0)),
            pl.BlockSpec((1, half), lambda i: (0, 0)),
            pl.BlockSpec(ws.shape, lambda i: (0, 0, 0)),
            pl.BlockSpec(bs_t.shape, lambda i: (0, 0)),
        ],
        out_specs=pl.BlockSpec((tm, half), lambda i: (i, 0)),
        out_shape=jax.ShapeDtypeStruct((t, half), BF16),
        scratch_shapes=[pltpu.VMEM((tm, half), BF16)],
        compiler_params=_params(("arbitrary",)),
        name="gate",
    )(z, z, ln_g, ln_b, ws, bs_t)


def _outproj_kernel(a_ref, w_ref, x_ref, gate_ref, o_ref, *, nk):
    if nk == 1:
        d = jnp.dot(a_ref[...], w_ref[...], preferred_element_type=F32)
        o_ref[...] = x_ref[...] + gate_ref[0] * d
        return

    @pl.when(pl.program_id(1) == 0)
    def _():
        o_ref[...] = x_ref[...]

    d = jnp.dot(a_ref[...], w_ref[...], preferred_element_type=F32)
    o_ref[...] += gate_ref[0] * d


def _outproj(a, w, x2, modt, gate_row, seq, tm=512, tk=2048):
    t, kdim = a.shape
    d = w.shape[1]
    tm = _tile(seq, tm)
    tk = _tile(kdim, tk)
    tpb = seq // tm
    nk = kdim // tk
    return pl.pallas_call(
        functools.partial(_outproj_kernel, nk=nk),
        grid=(t // tm, nk),
        in_specs=[
            pl.BlockSpec((tm, tk), lambda i, k: (i, k)),
            pl.BlockSpec((tk, d), lambda i, k: (k, 0)),
            pl.BlockSpec((tm, d), lambda i, k: (i, 0)),
            pl.BlockSpec((1, 1, d), lambda i, k: (gate_row(i // tpb), 0, 0)),
        ],
        out_specs=pl.BlockSpec((tm, d), lambda i, k: (i, 0)),
        out_shape=jax.ShapeDtypeStruct((t, d), F32),
        compiler_params=_params(("arbitrary", "arbitrary")),
        name="outproj",
    )(a, w, x2, modt)


def _mlp_kernel(x_ref, g_ref, sh_ref, sc_ref, xres_ref, gate_ref, w1_ref, w2_ref, o_ref,
                h_a, h_b, *, nj):
    @pl.when(pl.program_id(1) == 0)
    def _():
        o_ref[...] = xres_ref[...]

    def main(h):
        a = jnp.dot(h[...], w1_ref[...], preferred_element_type=F32)
        a = jnp.square(jnp.maximum(a, 0.0)).astype(BF16)
        o_ref[...] += gate_ref[0] * jnp.dot(a, w2_ref[...], preferred_element_type=F32)

    _ahead(x_ref, g_ref, sh_ref, sc_ref, h_a, h_b, nj, main)


def _mlp(x2, g, modt, sh_row, sc_row, gate_row, w1, w2, seq, tm=512, tf=1024):
    t, d = x2.shape
    dff = w1.shape[1]
    tm = _tile(seq, tm)
    tf = _tile(dff, tf)
    tpb = seq // tm
    n_tiles = t // tm
    nj = dff // tf
    return pl.pallas_call(
        functools.partial(_mlp_kernel, nj=nj),
        grid=(n_tiles + 1, nj),
        in_specs=_ahead_specs(d, tm, n_tiles, tpb, sh_row, sc_row) + [
            pl.BlockSpec((tm, d), lambda i, j: (_out_tile(i), 0)),
            pl.BlockSpec((1, 1, d), lambda i, j: (gate_row(_out_tile(i) // tpb), 0, 0)),
            pl.BlockSpec((d, tf), lambda i, j: (0, _col_step(i, j))),
            pl.BlockSpec((tf, d), lambda i, j: (_col_step(i, j), 0)),
        ],
        out_specs=pl.BlockSpec((tm, d), lambda i, j: (_out_tile(i), 0)),
        out_shape=jax.ShapeDtypeStruct((t, d), F32),
        scratch_shapes=[pltpu.VMEM((tm, d), BF16), pltpu.VMEM((tm, d), BF16)],
        compiler_params=_params(("arbitrary", "arbitrary")),
        name="mlp",
    )(x2, g, modt, modt, x2, modt, w1, w2)


def _cumsum_kernel(x_ref, o_ref):
    x = x_ref[...]
    n = x.shape[1]
    lane = lax.broadcasted_iota(jnp.int32, x.shape, 1)
    d = 1
    while d < n:
        x = x + jnp.where(lane >= d, pltpu.roll(x, d, axis=1), 0.0)
        d *= 2
    o_ref[...] = x


def _cumsum(x):
    return pl.pallas_call(
        _cumsum_kernel,
        out_shape=jax.ShapeDtypeStruct(x.shape, x.dtype),
        name="cumsum",
    )(x)


def _flash_kernel(q_ref, k_ref, vt_ref, f_ref, o_ref, frep_scr, sa_scr, sb_scr, *, tq):
    seq = k_ref.shape[2]
    tk = tq
    nq = seq // tq
    n_lane_groups = tq // LANES
    nt = (((1,), (1,)), ((), ()))

    f = f_ref[0, 0]
    for c in range(nq):
        fc = jnp.broadcast_to(f[:, c * tk:(c + 1) * tk], (LANES, tk))
        frep_scr[c * tk:(c + 1) * tk, :] = fc.T * LOG2E

    ones = jnp.ones((2 * SUBLANES, tk), BF16)
    key = lax.broadcasted_iota(jnp.int32, (tk, tq), 0)
    qry = lax.broadcasted_iota(jnp.int32, (tk, tq), 1)
    visible = key <= qry

    def scores(qi, c):
        q = q_ref[0, 0, qi * tq:(qi + 1) * tq, :]
        s = lax.dot_general(k_ref[0, 0, c * tk:(c + 1) * tk, :], q, nt,
                            preferred_element_type=F32)
        bias = frep_scr[qi * tq:qi * tq + 1, :] - frep_scr[c * tk:(c + 1) * tk, :]
        s = s + jnp.concatenate([bias] * n_lane_groups, axis=1)
        return jnp.where(visible, s, -jnp.inf) if c == qi else s

    steps = [(qi, c) for qi in range(nq) for c in range(qi + 1)]
    bufs = (sa_scr, sb_scr)
    bufs[0][...] = scores(*steps[0])
    m = acc = None
    for g, (qi, c) in enumerate(steps):
        if g + 1 < len(steps):
            bufs[(g + 1) % 2][...] = scores(*steps[g + 1])
        s_ref = bufs[g % 2]
        v_aug = jnp.concatenate([vt_ref[0, 0, c], ones], axis=0)
        m_chunk = jnp.max(s_ref[...], axis=0, keepdims=True)
        if c == 0:
            m = m_chunk
            p = jnp.exp2(s_ref[...] - m).astype(BF16)
            acc = jnp.dot(v_aug, p, preferred_element_type=F32)
        else:
            m_new = jnp.maximum(m, m_chunk)
            p = jnp.exp2(s_ref[...] - m_new).astype(BF16)
            acc = jnp.exp2(m - m_new) * acc + jnp.dot(v_aug, p, preferred_element_type=F32)
            m = m_new
        if c == qi:
            inv_l = 1.0 / acc[HEAD_DIM:HEAD_DIM + 1]
            o_ref[0, qi * tq:(qi + 1) * tq, :] = (acc[:HEAD_DIM] * inv_l).T.astype(o_ref.dtype)


def _flash(q, k, vt, fcum, tq):
    nb, nh, seq, dh = q.shape
    nq = seq // tq
    assert vt.shape == (nb, nh, nq, dh, tq)
    return pl.pallas_call(
        functools.partial(_flash_kernel, tq=tq),
        grid=(nb, nh),
        in_specs=[
            pl.BlockSpec((1, 1, seq, dh), lambda b, h: (b, h, 0, 0)),
            pl.BlockSpec((1, 1, seq, dh), lambda b, h: (b, h, 0, 0)),
            pl.BlockSpec((1, 1, nq, dh, tq), lambda b, h: (b, h, 0, 0, 0)),
            pl.BlockSpec((1, 1, 1, seq), lambda b, h: (b, h, 0, 0)),
        ],
        out_specs=pl.BlockSpec((1, seq, dh), lambda b, h: (b, 0, h)),
        out_shape=jax.ShapeDtypeStruct((nb, seq, nh * dh), BF16),
        scratch_shapes=[
            pltpu.VMEM((seq, LANES), F32),
            pltpu.VMEM((tq, tq), F32),
            pltpu.VMEM((tq, tq), F32),
        ],
        compiler_params=_params(("arbitrary", "arbitrary")),
        name="flash",
    )(q, k, vt, fcum.reshape(nb, nh, 1, seq))


def kernel(x, c, ada_w, ada_b, norm_g, mlp_w1, mlp_w2, gmlp_w_in, gmlp_ln_g, gmlp_ln_b,
           gmlp_ws, gmlp_bs, gmlp_w_out, kv_norm_g, kv_ada_w, kv_ada_b, w_kv, k_norm_g,
           w_f, b_f, attn_wq, q_norm_g, attn_wo):
    nb, seq, d = x.shape
    depth = ada_w.shape[0]
    n_a = gmlp_w_in.shape[0]
    n_heads = w_f.shape[1]

    modt = _mod(c, ada_w, ada_b).reshape(depth * nb * N_MOD, 1, d)
    kvmod = _mod(c, kv_ada_w[None], kv_ada_b[None]).reshape(nb * 2, 1, d)

    def mod_row(layer, which):
        return lambda b: (layer * nb + b) * N_MOD + which

    x2 = x.reshape(nb * seq, d)
    inv_sqrt = 1.0 / float(HEAD_DIM) ** 0.5
    k_heads = v_heads = fcum = None
    for layer in range(depth):
        g1 = norm_g[layer, 0][None]
        g2 = norm_g[layer, 1][None]
        if layer < n_a:
            a = layer
            z = _proj_gelu(x2, g1, modt, mod_row(layer, 0), mod_row(layer, 1),
                           gmlp_w_in[a].astype(BF16), seq)
            p = _gate(z, gmlp_ln_g[a][None], gmlp_ln_b[a][None], gmlp_ws[a], gmlp_bs[a].T)
            x2 = _outproj(p, gmlp_w_out[a].astype(BF16), x2, modt, mod_row(layer, 2), seq)
        else:
            if layer == n_a:
                kv_sh = lambda b: b * 2
                kv_sc = lambda b: b * 2 + 1
                kvg = kv_norm_g[None]
                k_heads = _proj_heads(x2, kvg, kvmod, kv_sh, kv_sc,
                                      w_kv[:, :d].astype(BF16), k_norm_g[None], seq, norm=True)
                v_heads = _proj_heads(x2, kvg, kvmod, kv_sh, kv_sc,
                                      w_kv[:, d:].astype(BF16), k_norm_g[None], seq, norm=False,
                                      transpose_chunk=FLASH_TILE)
                w_f_pad = jnp.pad(w_f, ((0, 0), (0, LANES - n_heads))).astype(BF16)
                b_f_pad = jnp.pad(b_f, (0, LANES - n_heads))[None]
                logf = _proj_forget(x2, kvg, kvmod, kv_sh, kv_sc, w_f_pad, b_f_pad,
                                    n_heads, seq)
                fcum = _cumsum(logf.reshape(nb * n_heads, seq)).reshape(nb, n_heads, seq)
            bl = layer - n_a
            q_heads = _proj_heads(x2, g1, modt, mod_row(layer, 0), mod_row(layer, 1),
                                  attn_wq[bl].astype(BF16), q_norm_g[bl][None], seq,
                                  norm=True, out_scale=inv_sqrt * LOG2E)
            o = _flash(q_heads, k_heads, v_heads, fcum, tq=FLASH_TILE)
            x2 = _outproj(o.reshape(nb * seq, d), attn_wo[bl].astype(BF16), x2, modt,
                          mod_row(layer, 2), seq)
        x2 = _mlp(x2, g2, modt, mod_row(layer, 3), mod_row(layer, 4), mod_row(layer, 5),
                  mlp_w1[layer].astype(BF16), mlp_w2[layer].astype(BF16), seq)
    return x2.reshape(nb, seq, d)
```

```python
import functools

import jax
import jax.numpy as jnp
from jax import lax
from jax.experimental import pallas as pl
from jax.experimental.pallas import tpu as pltpu

F32 = jnp.float32
BF16 = jnp.bfloat16

EPS = 1e-6
CHUNK = 64
GMLP_BLOCK = 128
GMLP_GROUPS = 8
HEAD_DIM = 128
N_MOD = 6
LANES = 128
SUBLANES = 8
MXU_DIM = 256
LOG2E = 1.4426950408889634
FLASH_TILE = 512

MIB = 1024 * 1024
VMEM_LIMIT = 56 * MIB


def _params(semantics, vmem=VMEM_LIMIT):
    return pltpu.CompilerParams(dimension_semantics=semantics, vmem_limit_bytes=vmem)


def _tile(n, target, align=MXU_DIM):
    best = None
    for t in range(align, min(n, target) + 1, align):
        if n % t == 0:
            best = t
    assert best is not None, (n, target, align)
    return best


def _adaln(x, g, shift, scale):
    ms = jnp.mean(x * x, axis=-1, keepdims=True)
    return (x * lax.rsqrt(ms + EPS)) * (g * (1.0 + scale)) + shift


def _gelu(a):
    return 0.5 * a * (1.0 + lax.erf(a * (0.5 ** 0.5)))


def _mod_kernel(c_ref, w_ref, b_ref, o_ref):
    sc = jax.nn.silu(c_ref[...])
    o_ref[0] = jnp.dot(sc, w_ref[0], preferred_element_type=F32) + b_ref[0]


def _mod(c, w, b, tn=1024):
    nl, d, n = w.shape
    nb = c.shape[0]
    tn = _tile(n, tn)
    return pl.pallas_call(
        _mod_kernel,
        grid=(nl, n // tn),
        in_specs=[
            pl.BlockSpec((nb, d), lambda l, j: (0, 0)),
            pl.BlockSpec((1, d, tn), lambda l, j: (l, 0, j)),
            pl.BlockSpec((1, 1, tn), lambda l, j: (l, 0, j)),
        ],
        out_specs=pl.BlockSpec((1, nb, tn), lambda l, j: (l, 0, j)),
        out_shape=jax.ShapeDtypeStruct((nl, nb, n), F32),
        compiler_params=_params(("arbitrary", "arbitrary")),
        name="mod",
    )(c, w, b.reshape(nl, 1, n))


def _cast_kernel(w_ref, o_ref):
    o_ref[...] = w_ref[...].astype(o_ref.dtype)


def _cast_bf16(w, block_elems=2 * 1024 * 1024):
    nl, r, c = w.shape
    tr = _tile(r, max(MXU_DIM, block_elems // c))
    return pl.pallas_call(
        _cast_kernel,
        grid=(nl, r // tr),
        in_specs=[pl.BlockSpec((1, tr, c), lambda l, i: (l, i, 0))],
        out_specs=pl.BlockSpec((1, tr, c), lambda l, i: (l, i, 0)),
        out_shape=jax.ShapeDtypeStruct(w.shape, BF16),
        compiler_params=_params(("arbitrary", "arbitrary")),
        name="cast",
    )(w)


def _ahead(x_ref, g_ref, sh_ref, sc_ref, h_a, h_b, nj, main):
    i = pl.program_id(0)
    j = pl.program_id(1)
    rows_per_step = x_ref.shape[0] // nj

    def fill(h_dst):
        rows = pl.ds(pl.multiple_of(j * rows_per_step, rows_per_step), rows_per_step)
        h = _adaln(x_ref[rows, :], g_ref[...], sh_ref[0], sc_ref[0])
        h_dst[rows, :] = h.astype(BF16)

    @pl.when(i == 0)
    def _():
        fill(h_a)

    @pl.when(jnp.logical_and(i > 0, i % 2 == 0))
    def _():
        fill(h_a)
        main(h_b)

    @pl.when(i % 2 == 1)
    def _():
        fill(h_b)
        main(h_a)


def _ahead_specs(d, tm, n_tiles, tpb, sh_row, sc_row):
    fill_tile = lambda i: jnp.minimum(i, n_tiles - 1)
    return [
        pl.BlockSpec((tm, d), lambda i, j: (fill_tile(i), 0)),
        pl.BlockSpec((1, d), lambda i, j: (0, 0)),
        pl.BlockSpec((1, 1, d), lambda i, j: (sh_row(fill_tile(i) // tpb), 0, 0)),
        pl.BlockSpec((1, 1, d), lambda i, j: (sc_row(fill_tile(i) // tpb), 0, 0)),
    ]


def _out_tile(i):
    return jnp.maximum(i - 1, 0)


def _col_step(i, j):
    return jnp.where(i == 0, 0, j)


def _proj_gelu_kernel(x_ref, g_ref, sh_ref, sc_ref, w_ref, o_ref, h_a, h_b, *, nj):
    def main(h):
        a = jnp.dot(h[...], w_ref[...], preferred_element_type=F32)
        o_ref[...] = _gelu(a).astype(o_ref.dtype)

    _ahead(x_ref, g_ref, sh_ref, sc_ref, h_a, h_b, nj, main)


def _proj_gelu(x2, g, modt, sh_row, sc_row, w, layer, seq, tm=1024, tn=1536):
    t, d = x2.shape
    n = w.shape[2]
    tm = _tile(seq, tm)
    tn = _tile(n, tn)
    tpb = seq // tm
    n_tiles = t // tm
    nj = n // tn
    return pl.pallas_call(
        functools.partial(_proj_gelu_kernel, nj=nj),
        grid=(n_tiles + 1, nj),
        in_specs=_ahead_specs(d, tm, n_tiles, tpb, sh_row, sc_row) + [
            pl.BlockSpec((None, d, tn), lambda i, j: (layer, 0, _col_step(i, j))),
        ],
        out_specs=pl.BlockSpec((tm, tn), lambda i, j: (_out_tile(i), _col_step(i, j))),
        out_shape=jax.ShapeDtypeStruct((t, n), BF16),
        scratch_shapes=[pltpu.VMEM((tm, d), BF16), pltpu.VMEM((tm, d), BF16)],
        compiler_params=_params(("arbitrary", "arbitrary")),
        name="proj_gelu",
    )(x2, g, modt, modt, w)


def _proj_heads_kernel(x_ref, g_ref, sh_ref, sc_ref, w_ref, hg_ref, o_ref, h_a, h_b, *,
                       nj, norm, out_scale, transpose_chunk):
    def main(h):
        a = jnp.dot(h[...], w_ref[...], preferred_element_type=F32)
        for hh in range(a.shape[1] // HEAD_DIM):
            t = a[:, hh * HEAD_DIM:(hh + 1) * HEAD_DIM]
            if norm:
                ms = jnp.mean(t * t, axis=-1, keepdims=True)
                t = t * lax.rsqrt(ms + EPS) * hg_ref[...]
            if out_scale != 1.0:
                t = t * out_scale
            if transpose_chunk:
                tt = t.T
                for cc in range(t.shape[0] // transpose_chunk):
                    cols = slice(cc * transpose_chunk, (cc + 1) * transpose_chunk)
                    o_ref[0, hh, cc] = tt[:, cols].astype(o_ref.dtype)
            else:
                o_ref[0, hh] = t.astype(o_ref.dtype)

    _ahead(x_ref, g_ref, sh_ref, sc_ref, h_a, h_b, nj, main)


def _proj_heads(x2, g, modt, sh_row, sc_row, w, layer, col0, n, head_g, seq, *, norm,
                out_scale=1.0, transpose_chunk=0, tm=512, tn=2048):
    t, d = x2.shape
    tm = _tile(seq, tm)
    tn = _tile(n, tn)
    nb = t // seq
    nh = n // HEAD_DIM
    tpb = seq // tm
    n_tiles = t // tm
    nj = n // tn
    col0_blocks = col0 // tn
    assert col0 % tn == 0
    hpt = tn // HEAD_DIM
    kern = functools.partial(_proj_heads_kernel, nj=nj, norm=norm, out_scale=out_scale,
                             transpose_chunk=transpose_chunk)
    if transpose_chunk:
        cpt = tm // transpose_chunk
        out_spec = pl.BlockSpec(
            (1, hpt, cpt, HEAD_DIM, transpose_chunk),
            lambda i, j: (_out_tile(i) // tpb, _col_step(i, j), _out_tile(i) % tpb, 0, 0))
        out_shape = (nb, nh, seq // transpose_chunk, HEAD_DIM, transpose_chunk)
    else:
        out_spec = pl.BlockSpec(
            (1, hpt, tm, HEAD_DIM),
            lambda i, j: (_out_tile(i) // tpb, _col_step(i, j), _out_tile(i) % tpb, 0))
        out_shape = (nb, nh, seq, HEAD_DIM)
    return pl.pallas_call(
        kern,
        grid=(n_tiles + 1, nj),
        in_specs=_ahead_specs(d, tm, n_tiles, tpb, sh_row, sc_row) + [
            pl.BlockSpec((None, d, tn), lambda i, j: (layer, 0, col0_blocks + _col_step(i, j))),
            pl.BlockSpec((1, HEAD_DIM), lambda i, j: (0, 0)),
        ],
        out_specs=out_spec,
        out_shape=jax.ShapeDtypeStruct(out_shape, BF16),
        scratch_shapes=[pltpu.VMEM((tm, d), BF16), pltpu.VMEM((tm, d), BF16)],
        compiler_params=_params(("arbitrary", "arbitrary")),
        name="proj_heads",
    )(x2, g, modt, modt, w, head_g)


def _proj_forget_kernel(x_ref, g_ref, sh_ref, sc_ref, w_ref, b_ref, o_ref, *, n_heads):
    h = _adaln(x_ref[...], g_ref[...], sh_ref[0], sc_ref[0]).astype(BF16)
    f = jnp.dot(h, w_ref[...], preferred_element_type=F32) + b_ref[...]
    o_ref[0] = jax.nn.log_sigmoid(f).T[:n_heads]


def _proj_forget(x2, g, modt, sh_row, sc_row, w_pad, b_pad, n_heads, seq, tm=1024):
    t, d = x2.shape
    tm = _tile(seq, tm)
    tpb = seq // tm
    kern = functools.partial(_proj_forget_kernel, n_heads=n_heads)
    return pl.pallas_call(
        kern,
        grid=(t // tm,),
        in_specs=[
            pl.BlockSpec((tm, d), lambda i: (i, 0)),
            pl.BlockSpec((1, d), lambda i: (0, 0)),
            pl.BlockSpec((1, 1, d), lambda i: (sh_row(i // tpb), 0, 0)),
            pl.BlockSpec((1, 1, d), lambda i: (sc_row(i // tpb), 0, 0)),
            pl.BlockSpec((d, LANES), lambda i: (0, 0)),
            pl.BlockSpec((1, LANES), lambda i: (0, 0)),
        ],
        out_specs=pl.BlockSpec((1, n_heads, tm), lambda i: (i // tpb, 0, i % tpb)),
        out_shape=jax.ShapeDtypeStruct((t // seq, n_heads, seq), F32),
        compiler_params=_params(("arbitrary",)),
        name="proj_forget",
    )(x2, g, modt, modt, w_pad, b_pad)


def _gate_kernel(u_ref, v_ref, lg_ref, lb_ref, ws_ref, bst_ref, o_ref, vn_scr):
    v = v_ref[...].astype(F32)
    mu = jnp.mean(v, axis=-1, keepdims=True)
    vc = v - mu
    var = jnp.mean(vc * vc, axis=-1, keepdims=True)
    vn_scr[...] = (vc * lax.rsqrt(var + EPS) * lg_ref[...] + lb_ref[...]).astype(BF16)

    row = lax.broadcasted_iota(jnp.int32, (GMLP_BLOCK, GMLP_BLOCK), 0) // CHUNK
    col = lax.broadcasted_iota(jnp.int32, (GMLP_BLOCK, GMLP_BLOCK), 1) // CHUNK
    keep = col <= row
    gdim = v.shape[1] // GMLP_GROUPS
    for gi in range(GMLP_GROUPS):
        w = jnp.where(keep, ws_ref[gi], 0.0).astype(BF16)
        bias = bst_ref[:, gi:gi + 1]
        cols = slice(gi * gdim, (gi + 1) * gdim)
        for nb in range(v.shape[0] // GMLP_BLOCK):
            rows = slice(nb * GMLP_BLOCK, (nb + 1) * GMLP_BLOCK)
            sv = jnp.dot(w, vn_scr[rows, cols], preferred_element_type=F32) + bias
            o_ref[rows, cols] = (u_ref[rows, cols].astype(F32) * sv).astype(o_ref.dtype)


def _gate(z, ln_g, ln_b, ws, bs_t, tm=256):
    t, n2 = z.shape
    half = n2 // 2
    return pl.pallas_call(
        _gate_kernel,
        grid=(t // tm,),
        in_specs=[
            pl.BlockSpec((tm, half), lambda i: (i, 0)),
            pl.BlockSpec((tm, half), lambda i: (i, 1)),
            pl.BlockSpec((1, half), lambda i: (0, 0)),
            pl.BlockSpec((1, half), lambda i: (0, 0)),
            pl.BlockSpec(ws.shape, lambda i: (0, 0, 0)),
            pl.BlockSpec(bs_t.shape, lambda i: (0, 0)),
        ],
        out_specs=pl.BlockSpec((tm, half), lambda i: (i, 0)),
        out_shape=jax.ShapeDtypeStruct((t, half), BF16),
        scratch_shapes=[pltpu.VMEM((tm, half), BF16)],
        compiler_params=_params(("arbitrary",)),
        name="gate",
    )(z, z, ln_g, ln_b, ws, bs_t)


def _outproj_kernel(a_ref, w_ref, x_ref, gate_ref, o_ref, *, nk):
    if nk == 1:
        d = jnp.dot(a_ref[...], w_ref[...], preferred_element_type=F32)
        o_ref[...] = x_ref[...] + gate_ref[0] * d
        return

    @pl.when(pl.program_id(1) == 0)
    def _():
        o_ref[...] = x_ref[...]

    d = jnp.dot(a_ref[...], w_ref[...], preferred_element_type=F32)
    o_ref[...] += gate_ref[0] * d


def _outproj(a, w, layer, x2, modt, gate_row, seq, tm=512, tk=2048):
    t, kdim = a.shape
    d = w.shape[2]
    tm = _tile(seq, tm)
    tk = _tile(kdim, tk)
    tpb = seq // tm
    nk = kdim // tk
    return pl.pallas_call(
        functools.partial(_outproj_kernel, nk=nk),
        grid=(t // tm, nk),
        in_specs=[
            pl.BlockSpec((tm, tk), lambda i, k: (i, k)),
            pl.BlockSpec((None, tk, d), lambda i, k: (layer, k, 0)),
            pl.BlockSpec((tm, d), lambda i, k: (i, 0)),
            pl.BlockSpec((1, 1, d), lambda i, k: (gate_row(i // tpb), 0, 0)),
        ],
        out_specs=pl.BlockSpec((tm, d), lambda i, k: (i, 0)),
        out_shape=jax.ShapeDtypeStruct((t, d), F32),
        compiler_params=_params(("arbitrary", "arbitrary")),
        name="outproj",
    )(a, w, x2, modt)


def _mlp_kernel(x_ref, g_ref, sh_ref, sc_ref, xres_ref, gate_ref, w1_ref, w2_ref, o_ref,
                h_a, h_b, *, nj):
    @pl.when(pl.program_id(1) == 0)
    def _():
        o_ref[...] = xres_ref[...]

    def main(h):
        a = jnp.dot(h[...], w1_ref[...], preferred_element_type=F32)
        a = jnp.square(jnp.maximum(a, 0.0)).astype(BF16)
        o_ref[...] += gate_ref[0] * jnp.dot(a, w2_ref[...], preferred_element_type=F32)

    _ahead(x_ref, g_ref, sh_ref, sc_ref, h_a, h_b, nj, main)


def _mlp(x2, g, modt, sh_row, sc_row, gate_row, w1, w2, layer, seq, tm=512, tf=1024):
    t, d = x2.shape
    dff = w1.shape[2]
    tm = _tile(seq, tm)
    tf = _tile(dff, tf)
    tpb = seq // tm
    n_tiles = t // tm
    nj = dff // tf
    return pl.pallas_call(
        functools.partial(_mlp_kernel, nj=nj),
        grid=(n_tiles + 1, nj),
        in_specs=_ahead_specs(d, tm, n_tiles, tpb, sh_row, sc_row) + [
            pl.BlockSpec((tm, d), lambda i, j: (_out_tile(i), 0)),
            pl.BlockSpec((1, 1, d), lambda i, j: (gate_row(_out_tile(i) // tpb), 0, 0)),
            pl.BlockSpec((None, d, tf), lambda i, j: (layer, 0, _col_step(i, j))),
            pl.BlockSpec((None, tf, d), lambda i, j: (layer, _col_step(i, j), 0)),
        ],
        out_specs=pl.BlockSpec((tm, d), lambda i, j: (_out_tile(i), 0)),
        out_shape=jax.ShapeDtypeStruct((t, d), F32),
        scratch_shapes=[pltpu.VMEM((tm, d), BF16), pltpu.VMEM((tm, d), BF16)],
        compiler_params=_params(("arbitrary", "arbitrary")),
        name="mlp",
    )(x2, g, modt, modt, x2, modt, w1, w2)


def _cumsum_kernel(x_ref, o_ref):
    x = x_ref[...]
    n = x.shape[1]
    lane = lax.broadcasted_iota(jnp.int32, x.shape, 1)
    d = 1
    while d < n:
        x = x + jnp.where(lane >= d, pltpu.roll(x, d, axis=1), 0.0)
        d *= 2
    o_ref[...] = x


def _cumsum(x):
    return pl.pallas_call(
        _cumsum_kernel,
        out_shape=jax.ShapeDtypeStruct(x.shape, x.dtype),
        name="cumsum",
    )(x)


def _flash_kernel(q_ref, k_ref, vt_ref, f_ref, o_ref, frep_scr, sa_scr, sb_scr, *, tq):
    seq = k_ref.shape[2]
    tk = tq
    nq = seq // tq
    n_lane_groups = tq // LANES
    nt = (((1,), (1,)), ((), ()))

    f = f_ref[0, 0]
    for c in range(nq):
        fc = jnp.broadcast_to(f[:, c * tk:(c + 1) * tk], (LANES, tk))
        frep_scr[c * tk:(c + 1) * tk, :] = fc.T * LOG2E

    ones = jnp.ones((2 * SUBLANES, tk), BF16)
    key = lax.broadcasted_iota(jnp.int32, (tk, tq), 0)
    qry = lax.broadcasted_iota(jnp.int32, (tk, tq), 1)
    visible = key <= qry

    def scores(qi, c):
        q = q_ref[0, 0, qi * tq:(qi + 1) * tq, :]
        s = lax.dot_general(k_ref[0, 0, c * tk:(c + 1) * tk, :], q, nt,
                            preferred_element_type=F32)
        bias = frep_scr[qi * tq:qi * tq + 1, :] - frep_scr[c * tk:(c + 1) * tk, :]
        s = s + jnp.concatenate([bias] * n_lane_groups, axis=1)
        return jnp.where(visible, s, -jnp.inf) if c == qi else s

    steps = [(qi, c) for qi in range(nq) for c in range(qi + 1)]
    bufs = (sa_scr, sb_scr)
    bufs[0][...] = scores(*steps[0])
    m = acc = None
    for g, (qi, c) in enumerate(steps):
        if g + 1 < len(steps):
            bufs[(g + 1) % 2][...] = scores(*steps[g + 1])
        s_ref = bufs[g % 2]
        v_aug = jnp.concatenate([vt_ref[0, 0, c], ones], axis=0)
        m_chunk = jnp.max(s_ref[...], axis=0, keepdims=True)
        if c == 0:
            m = m_chunk
            p = jnp.exp2(s_ref[...] - m).astype(BF16)
            acc = jnp.dot(v_aug, p, preferred_element_type=F32)
        else:
            m_new = jnp.maximum(m, m_chunk)
            p = jnp.exp2(s_ref[...] - m_new).astype(BF16)
            acc = jnp.exp2(m - m_new) * acc + jnp.dot(v_aug, p, preferred_element_type=F32)
            m = m_new
        if c == qi:
            inv_l = 1.0 / acc[HEAD_DIM:HEAD_DIM + 1]
            o_ref[0, qi * tq:(qi + 1) * tq, :] = (acc[:HEAD_DIM] * inv_l).T.astype(o_ref.dtype)


def _flash(q, k, vt, fcum, tq):
    nb, nh, seq, dh = q.shape
    nq = seq // tq
    assert vt.shape == (nb, nh, nq, dh, tq)
    return pl.pallas_call(
        functools.partial(_flash_kernel, tq=tq),
        grid=(nb, nh),
        in_specs=[
            pl.BlockSpec((1, 1, seq, dh), lambda b, h: (b, h, 0, 0)),
            pl.BlockSpec((1, 1, seq, dh), lambda b, h: (b, h, 0, 0)),
            pl.BlockSpec((1, 1, nq, dh, tq), lambda b, h: (b, h, 0, 0, 0)),
            pl.BlockSpec((1, 1, 1, seq), lambda b, h: (b, h, 0, 0)),
        ],
        out_specs=pl.BlockSpec((1, seq, dh), lambda b, h: (b, 0, h)),
        out_shape=jax.ShapeDtypeStruct((nb, seq, nh * dh), BF16),
        scratch_shapes=[
            pltpu.VMEM((seq, LANES), F32),
            pltpu.VMEM((tq, tq), F32),
            pltpu.VMEM((tq, tq), F32),
        ],
        compiler_params=_params(("arbitrary", "arbitrary")),
        name="flash",
    )(q, k, vt, fcum.reshape(nb, nh, 1, seq))


def kernel(x, c, ada_w, ada_b, norm_g, mlp_w1, mlp_w2, gmlp_w_in, gmlp_ln_g, gmlp_ln_b,
           gmlp_ws, gmlp_bs, gmlp_w_out, kv_norm_g, kv_ada_w, kv_ada_b, w_kv, k_norm_g,
           w_f, b_f, attn_wq, q_norm_g, attn_wo):
    nb, seq, d = x.shape
    depth = ada_w.shape[0]
    n_a = gmlp_w_in.shape[0]
    n_heads = w_f.shape[1]

    modt = _mod(c, ada_w, ada_b).reshape(depth * nb * N_MOD, 1, d)
    kvmod = _mod(c, kv_ada_w[None], kv_ada_b[None]).reshape(nb * 2, 1, d)

    def mod_row(layer, which):
        return lambda b: (layer * nb + b) * N_MOD + which

    w_in = _cast_bf16(gmlp_w_in)
    w_out = _cast_bf16(gmlp_w_out)
    w1 = _cast_bf16(mlp_w1)
    w2 = _cast_bf16(mlp_w2)
    wkv = _cast_bf16(w_kv[None])
    wq = _cast_bf16(attn_wq)
    wo = _cast_bf16(attn_wo)

    x2 = x.reshape(nb * seq, d)
    inv_sqrt = 1.0 / float(HEAD_DIM) ** 0.5
    k_heads = v_heads = fcum = None
    for layer in range(depth):
        g1 = norm_g[layer, 0][None]
        g2 = norm_g[layer, 1][None]
        if layer < n_a:
            a = layer
            z = _proj_gelu(x2, g1, modt, mod_row(layer, 0), mod_row(layer, 1), w_in, a, seq)
            p = _gate(z, gmlp_ln_g[a][None], gmlp_ln_b[a][None], gmlp_ws[a], gmlp_bs[a].T)
            x2 = _outproj(p, w_out, a, x2, modt, mod_row(layer, 2), seq)
        else:
            if layer == n_a:
                kv_sh = lambda b: b * 2
                kv_sc = lambda b: b * 2 + 1
                kvg = kv_norm_g[None]
                k_heads = _proj_heads(x2, kvg, kvmod, kv_sh, kv_sc, wkv, 0, 0, d,
                                      k_norm_g[None], seq, norm=True)
                v_heads = _proj_heads(x2, kvg, kvmod, kv_sh, kv_sc, wkv, 0, d, d,
                                      k_norm_g[None], seq, norm=False,
                                      transpose_chunk=FLASH_TILE)
                w_f_pad = jnp.pad(w_f, ((0, 0), (0, LANES - n_heads))).astype(BF16)
                b_f_pad = jnp.pad(b_f, (0, LANES - n_heads))[None]
                logf = _proj_forget(x2, kvg, kvmod, kv_sh, kv_sc, w_f_pad, b_f_pad,
                                    n_heads, seq)
                fcum = _cumsum(logf.reshape(nb * n_heads, seq)).reshape(nb, n_heads, seq)
            bl = layer - n_a
            q_heads = _proj_heads(x2, g1, modt, mod_row(layer, 0), mod_row(layer, 1),
                                  wq, bl, 0, d, q_norm_g[bl][None], seq,
                                  norm=True, out_scale=inv_sqrt * LOG2E)
            o = _flash(q_heads, k_heads, v_heads, fcum, tq=FLASH_TILE)
            x2 = _outproj(o.reshape(nb * seq, d), wo, bl, x2, modt, mod_row(layer, 2), seq)
        x2 = _mlp(x2, g2, modt, mod_row(layer, 3), mod_row(layer, 4), mod_row(layer, 5),
                  w1, w2, layer, seq)
    return x2.reshape(nb, seq, d)
```

```python
import functools

import jax
import jax.numpy as jnp
from jax import lax
from jax.experimental import pallas as pl
from jax.experimental.pallas import tpu as pltpu

F32 = jnp.float32
BF16 = jnp.bfloat16

EPS = 1e-6
CHUNK = 64
GMLP_BLOCK = 128
GMLP_GROUPS = 8
HEAD_DIM = 128
N_MOD = 6
LANES = 128
SUBLANES = 8
MXU_DIM = 256
LOG2E = 1.4426950408889634
FLASH_TILE = 512

MIB = 1024 * 1024
VMEM_LIMIT = 56 * MIB


def _params(semantics, vmem=VMEM_LIMIT):
    return pltpu.CompilerParams(dimension_semantics=semantics, vmem_limit_bytes=vmem)


def _tile(n, target, align=MXU_DIM):
    best = None
    for t in range(align, min(n, target) + 1, align):
        if n % t == 0:
            best = t
    assert best is not None, (n, target, align)
    return best


def _adaln(x, g, shift, scale):
    ms = jnp.mean(x * x, axis=-1, keepdims=True)
    return (x * lax.rsqrt(ms + EPS)) * (g * (1.0 + scale)) + shift


def _gelu(a):
    return 0.5 * a * (1.0 + lax.erf(a * (0.5 ** 0.5)))


def _mod_kernel(c_ref, w_ref, b_ref, o_ref):
    sc = jax.nn.silu(c_ref[...])
    o_ref[0] = jnp.dot(sc, w_ref[0], preferred_element_type=F32) + b_ref[0]


def _mod(c, w, b, tn=1024):
    nl, d, n = w.shape
    nb = c.shape[0]
    tn = _tile(n, tn)
    return pl.pallas_call(
        _mod_kernel,
        grid=(nl, n // tn),
        in_specs=[
            pl.BlockSpec((nb, d), lambda l, j: (0, 0)),
            pl.BlockSpec((1, d, tn), lambda l, j: (l, 0, j)),
            pl.BlockSpec((1, 1, tn), lambda l, j: (l, 0, j)),
        ],
        out_specs=pl.BlockSpec((1, nb, tn), lambda l, j: (l, 0, j)),
        out_shape=jax.ShapeDtypeStruct((nl, nb, n), F32),
        compiler_params=_params(("arbitrary", "arbitrary")),
        name="mod",
    )(c, w, b.reshape(nl, 1, n))


def _cast_kernel(w_ref, o_ref):
    o_ref[...] = w_ref[...].astype(o_ref.dtype)


def _cast_bf16(w, block_elems=2 * 1024 * 1024):
    nl, r, c = w.shape
    tr = _tile(r, max(MXU_DIM, block_elems // c))
    return pl.pallas_call(
        _cast_kernel,
        grid=(nl, r // tr),
        in_specs=[pl.BlockSpec((1, tr, c), lambda l, i: (l, i, 0))],
        out_specs=pl.BlockSpec((1, tr, c), lambda l, i: (l, i, 0)),
        out_shape=jax.ShapeDtypeStruct(w.shape, BF16),
        compiler_params=_params(("arbitrary", "arbitrary")),
        name="cast",
    )(w)


def _one_tile_ahead(fill, main, buf_a, buf_b):
    i = pl.program_id(0)

    @pl.when(i == 0)
    def _():
        fill(buf_a)

    @pl.when(jnp.logical_and(i > 0, i % 2 == 0))
    def _():
        main(buf_b)
        fill(buf_a)

    @pl.when(i % 2 == 1)
    def _():
        main(buf_a)
        fill(buf_b)


def _ahead(x_ref, g_ref, sh_ref, sc_ref, h_a, h_b, nj, main):
    j = pl.program_id(1)
    rows_per_step = x_ref.shape[0] // nj

    def fill(h_dst):
        rows = pl.ds(pl.multiple_of(j * rows_per_step, rows_per_step), rows_per_step)
        h = _adaln(x_ref[rows, :], g_ref[...], sh_ref[0], sc_ref[0])
        h_dst[rows, :] = h.astype(BF16)

    _one_tile_ahead(fill, main, h_a, h_b)


def _ahead_specs(d, tm, n_tiles, tpb, sh_row, sc_row):
    fill_tile = lambda i: jnp.minimum(i, n_tiles - 1)
    return [
        pl.BlockSpec((tm, d), lambda i, j: (fill_tile(i), 0)),
        pl.BlockSpec((1, d), lambda i, j: (0, 0)),
        pl.BlockSpec((1, 1, d), lambda i, j: (sh_row(fill_tile(i) // tpb), 0, 0)),
        pl.BlockSpec((1, 1, d), lambda i, j: (sc_row(fill_tile(i) // tpb), 0, 0)),
    ]


def _out_tile(i):
    return jnp.maximum(i - 1, 0)


def _col_step(i, j):
    return jnp.where(i == 0, 0, j)


def _proj_gelu_kernel(x_ref, g_ref, sh_ref, sc_ref, w_ref, o_ref, h_a, h_b, *, nj):
    def main(h):
        a = jnp.dot(h[...], w_ref[...], preferred_element_type=F32)
        o_ref[...] = _gelu(a).astype(o_ref.dtype)

    _ahead(x_ref, g_ref, sh_ref, sc_ref, h_a, h_b, nj, main)


def _proj_gelu(x2, g, modt, sh_row, sc_row, w, layer, seq, tm=1024, tn=1536):
    t, d = x2.shape
    n = w.shape[2]
    tm = _tile(seq, tm)
    tn = _tile(n, tn)
    tpb = seq // tm
    n_tiles = t // tm
    nj = n // tn
    return pl.pallas_call(
        functools.partial(_proj_gelu_kernel, nj=nj),
        grid=(n_tiles + 1, nj),
        in_specs=_ahead_specs(d, tm, n_tiles, tpb, sh_row, sc_row) + [
            pl.BlockSpec((None, d, tn), lambda i, j: (layer, 0, _col_step(i, j))),
        ],
        out_specs=pl.BlockSpec((tm, tn), lambda i, j: (_out_tile(i), _col_step(i, j))),
        out_shape=jax.ShapeDtypeStruct((t, n), BF16),
        scratch_shapes=[pltpu.VMEM((tm, d), BF16), pltpu.VMEM((tm, d), BF16)],
        compiler_params=_params(("arbitrary", "arbitrary")),
        name="proj_gelu",
    )(x2, g, modt, modt, w)


def _proj_heads_kernel(x_ref, g_ref, sh_ref, sc_ref, w_ref, hg_ref, o_ref, h_a, h_b, *,
                       nj, norm, out_scale, transpose_chunk):
    def main(h):
        a = jnp.dot(h[...], w_ref[...], preferred_element_type=F32)
        for hh in range(a.shape[1] // HEAD_DIM):
            t = a[:, hh * HEAD_DIM:(hh + 1) * HEAD_DIM]
            if norm:
                ms = jnp.mean(t * t, axis=-1, keepdims=True)
                t = t * lax.rsqrt(ms + EPS) * hg_ref[...]
            if out_scale != 1.0:
                t = t * out_scale
            if transpose_chunk:
                tt = t.T
                for cc in range(t.shape[0] // transpose_chunk):
                    cols = slice(cc * transpose_chunk, (cc + 1) * transpose_chunk)
                    o_ref[0, hh, cc] = tt[:, cols].astype(o_ref.dtype)
            else:
                o_ref[0, hh] = t.astype(o_ref.dtype)

    _ahead(x_ref, g_ref, sh_ref, sc_ref, h_a, h_b, nj, main)


def _proj_heads(x2, g, modt, sh_row, sc_row, w, layer, col0, n, head_g, seq, *, norm,
                out_scale=1.0, transpose_chunk=0, tm=512, tn=2048):
    t, d = x2.shape
    tm = _tile(seq, tm)
    tn = _tile(n, tn)
    nb = t // seq
    nh = n // HEAD_DIM
    tpb = seq // tm
    n_tiles = t // tm
    nj = n // tn
    col0_blocks = col0 // tn
    assert col0 % tn == 0
    hpt = tn // HEAD_DIM
    kern = functools.partial(_proj_heads_kernel, nj=nj, norm=norm, out_scale=out_scale,
                             transpose_chunk=transpose_chunk)
    if transpose_chunk:
        cpt = tm // transpose_chunk
        out_spec = pl.BlockSpec(
            (1, hpt, cpt, HEAD_DIM, transpose_chunk),
            lambda i, j: (_out_tile(i) // tpb, _col_step(i, j), _out_tile(i) % tpb, 0, 0))
        out_shape = (nb, nh, seq // transpose_chunk, HEAD_DIM, transpose_chunk)
    else:
        out_spec = pl.BlockSpec(
            (1, hpt, tm, HEAD_DIM),
            lambda i, j: (_out_tile(i) // tpb, _col_step(i, j), _out_tile(i) % tpb, 0))
        out_shape = (nb, nh, seq, HEAD_DIM)
    return pl.pallas_call(
        kern,
        grid=(n_tiles + 1, nj),
        in_specs=_ahead_specs(d, tm, n_tiles, tpb, sh_row, sc_row) + [
            pl.BlockSpec((None, d, tn), lambda i, j: (layer, 0, col0_blocks + _col_step(i, j))),
            pl.BlockSpec((1, HEAD_DIM), lambda i, j: (0, 0)),
        ],
        out_specs=out_spec,
        out_shape=jax.ShapeDtypeStruct(out_shape, BF16),
        scratch_shapes=[pltpu.VMEM((tm, d), BF16), pltpu.VMEM((tm, d), BF16)],
        compiler_params=_params(("arbitrary", "arbitrary")),
        name="proj_heads",
    )(x2, g, modt, modt, w, head_g)


def _proj_forget_kernel(x_ref, g_ref, sh_ref, sc_ref, w_ref, b_ref, o_ref, *, n_heads):
    h = _adaln(x_ref[...], g_ref[...], sh_ref[0], sc_ref[0]).astype(BF16)
    f = jnp.dot(h, w_ref[...], preferred_element_type=F32) + b_ref[...]
    o_ref[0] = jax.nn.log_sigmoid(f).T[:n_heads]


def _proj_forget(x2, g, modt, sh_row, sc_row, w_pad, b_pad, n_heads, seq, tm=1024):
    t, d = x2.shape
    tm = _tile(seq, tm)
    tpb = seq // tm
    kern = functools.partial(_proj_forget_kernel, n_heads=n_heads)
    return pl.pallas_call(
        kern,
        grid=(t // tm,),
        in_specs=[
            pl.BlockSpec((tm, d), lambda i: (i, 0)),
            pl.BlockSpec((1, d), lambda i: (0, 0)),
            pl.BlockSpec((1, 1, d), lambda i: (sh_row(i // tpb), 0, 0)),
            pl.BlockSpec((1, 1, d), lambda i: (sc_row(i // tpb), 0, 0)),
            pl.BlockSpec((d, LANES), lambda i: (0, 0)),
            pl.BlockSpec((1, LANES), lambda i: (0, 0)),
        ],
        out_specs=pl.BlockSpec((1, n_heads, tm), lambda i: (i // tpb, 0, i % tpb)),
        out_shape=jax.ShapeDtypeStruct((t // seq, n_heads, seq), F32),
        compiler_params=_params(("arbitrary",)),
        name="proj_forget",
    )(x2, g, modt, modt, w_pad, b_pad)


def _gate_outproj_kernel(u_ref, v_ref, lg_ref, lb_ref, ws_ref, bst_ref, w_ref, x_ref, gate_ref,
                         o_ref, vn_scr, p_a, p_b):
    k = pl.program_id(1)
    nk, _, tk = p_a.shape
    half = v_ref.shape[1]
    gdim = half // GMLP_GROUPS

    @pl.when(k == 0)
    def _():
        o_ref[...] = x_ref[...]

    def fill(p_dst):
        v = v_ref[...].astype(F32)
        mu = jnp.mean(v, axis=-1, keepdims=True)
        vc = v - mu
        var = jnp.mean(vc * vc, axis=-1, keepdims=True)
        vn_scr[...] = (vc * lax.rsqrt(var + EPS) * lg_ref[...] + lb_ref[...]).astype(BF16)
        row = lax.broadcasted_iota(jnp.int32, (GMLP_BLOCK, GMLP_BLOCK), 0) // CHUNK
        col = lax.broadcasted_iota(jnp.int32, (GMLP_BLOCK, GMLP_BLOCK), 1) // CHUNK
        keep = col <= row
        rows = pl.ds(pl.multiple_of(k * GMLP_BLOCK, GMLP_BLOCK), GMLP_BLOCK)
        for gi in range(GMLP_GROUPS):
            ws = jnp.where(keep, ws_ref[gi], 0.0).astype(BF16)
            cols = slice(gi * gdim, (gi + 1) * gdim)
            sv = jnp.dot(ws, vn_scr[:, cols], preferred_element_type=F32) + bst_ref[:, gi:gi + 1]
            p = (u_ref[:, cols].astype(F32) * sv).astype(BF16)
            k_slice, off = divmod(gi * gdim, tk)
            p_dst[k_slice, rows, off:off + gdim] = p

    def main(p_src):
        d = jnp.dot(p_src[k], w_ref[...], preferred_element_type=F32)
        o_ref[...] += gate_ref[0] * d

    _one_tile_ahead(fill, main, p_a, p_b)


def _gate_outproj(z, ln_g, ln_b, ws, bs_t, w, layer, x2, modt, gate_row, seq, tk=1536):
    t, n2 = z.shape
    half = n2 // 2
    d = w.shape[2]
    tk = _tile(half, tk)
    nk = half // tk
    tm = nk * GMLP_BLOCK
    assert seq % tm == 0 and tk % (half // GMLP_GROUPS) == 0
    tpb = seq // tm
    n_tiles = t // tm
    fill_block = lambda i, k: jnp.minimum(i, n_tiles - 1) * nk + k
    return pl.pallas_call(
        _gate_outproj_kernel,
        grid=(n_tiles + 1, nk),
        in_specs=[
            pl.BlockSpec((GMLP_BLOCK, half), lambda i, k: (fill_block(i, k), 0)),
            pl.BlockSpec((GMLP_BLOCK, half), lambda i, k: (fill_block(i, k), 1)),
            pl.BlockSpec((1, half), lambda i, k: (0, 0)),
            pl.BlockSpec((1, half), lambda i, k: (0, 0)),
            pl.BlockSpec(ws.shape, lambda i, k: (0, 0, 0)),
            pl.BlockSpec(bs_t.shape, lambda i, k: (0, 0)),
            pl.BlockSpec((None, tk, d), lambda i, k: (layer, _col_step(i, k), 0)),
            pl.BlockSpec((tm, d), lambda i, k: (_out_tile(i), 0)),
            pl.BlockSpec((1, 1, d), lambda i, k: (gate_row(_out_tile(i) // tpb), 0, 0)),
        ],
        out_specs=pl.BlockSpec((tm, d), lambda i, k: (_out_tile(i), 0)),
        out_shape=jax.ShapeDtypeStruct((t, d), F32),
        scratch_shapes=[
            pltpu.VMEM((GMLP_BLOCK, half), BF16),
            pltpu.VMEM((nk, tm, tk), BF16),
            pltpu.VMEM((nk, tm, tk), BF16),
        ],
        compiler_params=_params(("arbitrary", "arbitrary")),
        name="gate_outproj",
    )(z, z, ln_g, ln_b, ws, bs_t, w, x2, modt)


def _outproj_kernel(a_ref, w_ref, x_ref, gate_ref, o_ref, *, nk):
    if nk == 1:
        d = jnp.dot(a_ref[...], w_ref[...], preferred_element_type=F32)
        o_ref[...] = x_ref[...] + gate_ref[0] * d
        return

    @pl.when(pl.program_id(1) == 0)
    def _():
        o_ref[...] = x_ref[...]

    d = jnp.dot(a_ref[...], w_ref[...], preferred_element_type=F32)
    o_ref[...] += gate_ref[0] * d


def _outproj(a, w, layer, x2, modt, gate_row, seq, tm=512, tk=2048):
    t, kdim = a.shape
    d = w.shape[2]
    tm = _tile(seq, tm)
    tk = _tile(kdim, tk)
    tpb = seq // tm
    nk = kdim // tk
    return pl.pallas_call(
        functools.partial(_outproj_kernel, nk=nk),
        grid=(t // tm, nk),
        in_specs=[
            pl.BlockSpec((tm, tk), lambda i, k: (i, k)),
            pl.BlockSpec((None, tk, d), lambda i, k: (layer, k, 0)),
            pl.BlockSpec((tm, d), lambda i, k: (i, 0)),
            pl.BlockSpec((1, 1, d), lambda i, k: (gate_row(i // tpb), 0, 0)),
        ],
        out_specs=pl.BlockSpec((tm, d), lambda i, k: (i, 0)),
        out_shape=jax.ShapeDtypeStruct((t, d), F32),
        compiler_params=_params(("arbitrary", "arbitrary")),
        name="outproj",
    )(a, w, x2, modt)


def _mlp_kernel(x_ref, g_ref, sh_ref, sc_ref, xres_ref, gate_ref, w1_ref, w2_ref, o_ref,
                h_a, h_b, *, nj):
    @pl.when(pl.program_id(1) == 0)
    def _():
        o_ref[...] = xres_ref[...]

    def main(h):
        a = jnp.dot(h[...], w1_ref[...], preferred_element_type=F32)
        a = jnp.square(jnp.maximum(a, 0.0)).astype(BF16)
        o_ref[...] += gate_ref[0] * jnp.dot(a, w2_ref[...], preferred_element_type=F32)

    _ahead(x_ref, g_ref, sh_ref, sc_ref, h_a, h_b, nj, main)


def _mlp(x2, g, modt, sh_row, sc_row, gate_row, w1, w2, layer, seq, tm=512, tf=1024):
    t, d = x2.shape
    dff = w1.shape[2]
    tm = _tile(seq, tm)
    tf = _tile(dff, tf)
    tpb = seq // tm
    n_tiles = t // tm
    nj = dff // tf
    return pl.pallas_call(
        functools.partial(_mlp_kernel, nj=nj),
        grid=(n_tiles + 1, nj),
        in_specs=_ahead_specs(d, tm, n_tiles, tpb, sh_row, sc_row) + [
            pl.BlockSpec((tm, d), lambda i, j: (_out_tile(i), 0)),
            pl.BlockSpec((1, 1, d), lambda i, j: (gate_row(_out_tile(i) // tpb), 0, 0)),
            pl.BlockSpec((None, d, tf), lambda i, j: (layer, 0, _col_step(i, j))),
            pl.BlockSpec((None, tf, d), lambda i, j: (layer, _col_step(i, j), 0)),
        ],
        out_specs=pl.BlockSpec((tm, d), lambda i, j: (_out_tile(i), 0)),
        out_shape=jax.ShapeDtypeStruct((t, d), F32),
        scratch_shapes=[pltpu.VMEM((tm, d), BF16), pltpu.VMEM((tm, d), BF16)],
        compiler_params=_params(("arbitrary", "arbitrary")),
        name="mlp",
    )(x2, g, modt, modt, x2, modt, w1, w2)


def _cumsum_kernel(x_ref, o_ref):
    x = x_ref[...]
    n = x.shape[1]
    lane = lax.broadcasted_iota(jnp.int32, x.shape, 1)
    d = 1
    while d < n:
        x = x + jnp.where(lane >= d, pltpu.roll(x, d, axis=1), 0.0)
        d *= 2
    o_ref[...] = x


def _cumsum(x):
    return pl.pallas_call(
        _cumsum_kernel,
        out_shape=jax.ShapeDtypeStruct(x.shape, x.dtype),
        name="cumsum",
    )(x)


def _flash_kernel(q_ref, k_ref, vt_ref, f_ref, o_ref, frep_scr, sa_scr, sb_scr, *, tq):
    seq = k_ref.shape[2]
    tk = tq
    nq = seq // tq
    n_lane_groups = tq // LANES
    nt = (((1,), (1,)), ((), ()))

    f = f_ref[0, 0]
    for c in range(nq):
        fc = jnp.broadcast_to(f[:, c * tk:(c + 1) * tk], (LANES, tk))
        frep_scr[c * tk:(c + 1) * tk, :] = fc.T * LOG2E

    ones = jnp.ones((2 * SUBLANES, tk), BF16)
    key = lax.broadcasted_iota(jnp.int32, (tk, tq), 0)
    qry = lax.broadcasted_iota(jnp.int32, (tk, tq), 1)
    visible = key <= qry

    def scores(qi, c):
        q = q_ref[0, 0, qi * tq:(qi + 1) * tq, :]
        s = lax.dot_general(k_ref[0, 0, c * tk:(c + 1) * tk, :], q, nt,
                            preferred_element_type=F32)
        bias = frep_scr[qi * tq:qi * tq + 1, :] - frep_scr[c * tk:(c + 1) * tk, :]
        s = s + jnp.concatenate([bias] * n_lane_groups, axis=1)
        s = jnp.where(visible, s, -jnp.inf) if c == qi else s
        return s, jnp.max(s, axis=0, keepdims=True)

    steps = [(qi, c) for qi in range(nq) for c in range(qi + 1)]
    bufs = (sa_scr, sb_scr)
    bufs[0][...], m_next = scores(*steps[0])
    m = acc = None
    for g, (qi, c) in enumerate(steps):
        m_chunk = m_next
        if g + 1 < len(steps):
            bufs[(g + 1) % 2][...], m_next = scores(*steps[g + 1])
        s_ref = bufs[g % 2]
        v_aug = jnp.concatenate([vt_ref[0, 0, c], ones], axis=0)
        if c == 0:
            m = m_chunk
            p = jnp.exp2(s_ref[...] - m).astype(BF16)
            acc = jnp.dot(v_aug, p, preferred_element_type=F32)
        else:
            m_new = jnp.maximum(m, m_chunk)
            p = jnp.exp2(s_ref[...] - m_new).astype(BF16)
            acc = jnp.exp2(m - m_new) * acc + jnp.dot(v_aug, p, preferred_element_type=F32)
            m = m_new
        if c == qi:
            inv_l = 1.0 / acc[HEAD_DIM:HEAD_DIM + 1]
            o_ref[0, qi * tq:(qi + 1) * tq, :] = (acc[:HEAD_DIM] * inv_l).T.astype(o_ref.dtype)


def _flash(q, k, vt, fcum, tq):
    nb, nh, seq, dh = q.shape
    nq = seq // tq
    assert vt.shape == (nb, nh, nq, dh, tq)
    return pl.pallas_call(
        functools.partial(_flash_kernel, tq=tq),
        grid=(nb, nh),
        in_specs=[
            pl.BlockSpec((1, 1, seq, dh), lambda b, h: (b, h, 0, 0)),
            pl.BlockSpec((1, 1, seq, dh), lambda b, h: (b, h, 0, 0)),
            pl.BlockSpec((1, 1, nq, dh, tq), lambda b, h: (b, h, 0, 0, 0)),
            pl.BlockSpec((1, 1, 1, seq), lambda b, h: (b, h, 0, 0)),
        ],
        out_specs=pl.BlockSpec((1, seq, dh), lambda b, h: (b, 0, h)),
        out_shape=jax.ShapeDtypeStruct((nb, seq, nh * dh), BF16),
        scratch_shapes=[
            pltpu.VMEM((seq, LANES), F32),
            pltpu.VMEM((tq, tq), F32),
            pltpu.VMEM((tq, tq), F32),
        ],
        compiler_params=_params(("arbitrary", "arbitrary")),
        name="flash",
    )(q, k, vt, fcum.reshape(nb, nh, 1, seq))


def kernel(x, c, ada_w, ada_b, norm_g, mlp_w1, mlp_w2, gmlp_w_in, gmlp_ln_g, gmlp_ln_b,
           gmlp_ws, gmlp_bs, gmlp_w_out, kv_norm_g, kv_ada_w, kv_ada_b, w_kv, k_norm_g,
           w_f, b_f, attn_wq, q_norm_g, attn_wo):
    nb, seq, d = x.shape
    depth = ada_w.shape[0]
    n_a = gmlp_w_in.shape[0]
    n_heads = w_f.shape[1]

    modt = _mod(c, ada_w, ada_b).reshape(depth * nb * N_MOD, 1, d)
    kvmod = _mod(c, kv_ada_w[None], kv_ada_b[None]).reshape(nb * 2, 1, d)

    def mod_row(layer, which):
        return lambda b: (layer * nb + b) * N_MOD + which

    w_in = _cast_bf16(gmlp_w_in)
    w_out = _cast_bf16(gmlp_w_out)
    w1 = _cast_bf16(mlp_w1)
    w2 = _cast_bf16(mlp_w2)
    wkv = _cast_bf16(w_kv[None])
    wq = _cast_bf16(attn_wq)
    wo = _cast_bf16(attn_wo)

    x2 = x.reshape(nb * seq, d)
    inv_sqrt = 1.0 / float(HEAD_DIM) ** 0.5
    k_heads = v_heads = fcum = None
    for layer in range(depth):
        g1 = norm_g[layer, 0][None]
        g2 = norm_g[layer, 1][None]
        if layer < n_a:
            a = layer
            z = _proj_gelu(x2, g1, modt, mod_row(layer, 0), mod_row(layer, 1), w_in, a, seq)
            x2 = _gate_outproj(z, gmlp_ln_g[a][None], gmlp_ln_b[a][None], gmlp_ws[a],
                               gmlp_bs[a].T, w_out, a, x2, modt, mod_row(layer, 2), seq)
        else:
            if layer == n_a:
                kv_sh = lambda b: b * 2
                kv_sc = lambda b: b * 2 + 1
                kvg = kv_norm_g[None]
                k_heads = _proj_heads(x2, kvg, kvmod, kv_sh, kv_sc, wkv, 0, 0, d,
                                      k_norm_g[None], seq, norm=True)
                v_heads = _proj_heads(x2, kvg, kvmod, kv_sh, kv_sc, wkv, 0, d, d,
                                      k_norm_g[None], seq, norm=False,
                                      transpose_chunk=FLASH_TILE)
                w_f_pad = jnp.pad(w_f, ((0, 0), (0, LANES - n_heads))).astype(BF16)
                b_f_pad = jnp.pad(b_f, (0, LANES - n_heads))[None]
                logf = _proj_forget(x2, kvg, kvmod, kv_sh, kv_sc, w_f_pad, b_f_pad,
                                    n_heads, seq)
                fcum = _cumsum(logf.reshape(nb * n_heads, seq)).reshape(nb, n_heads, seq)
            bl = layer - n_a
            q_heads = _proj_heads(x2, g1, modt, mod_row(layer, 0), mod_row(layer, 1),
                                  wq, bl, 0, d, q_norm_g[bl][None], seq,
                                  norm=True, out_scale=inv_sqrt * LOG2E)
            o = _flash(q_heads, k_heads, v_heads, fcum, tq=FLASH_TILE)
            x2 = _outproj(o.reshape(nb * seq, d), wo, bl, x2, modt, mod_row(layer, 2), seq)
        x2 = _mlp(x2, g2, modt, mod_row(layer, 3), mod_row(layer, 4), mod_row(layer, 5),
                  w1, w2, layer, seq)
    return x2.reshape(nb, seq, d)
```

```python
import functools

import jax
import jax.numpy as jnp
from jax import lax
from jax.experimental import pallas as pl
from jax.experimental.pallas import tpu as pltpu

F32 = jnp.float32
BF16 = jnp.bfloat16

EPS = 1e-6
CHUNK = 64
GMLP_BLOCK = 128
GMLP_GROUPS = 8
HEAD_DIM = 128
N_MOD = 6
LANES = 128
SUBLANES = 8
MXU_DIM = 256
LOG2E = 1.4426950408889634
FLASH_TILE = 512

MIB = 1024 * 1024
VMEM_LIMIT = 56 * MIB


def _params(semantics, vmem=VMEM_LIMIT):
    return pltpu.CompilerParams(dimension_semantics=semantics, vmem_limit_bytes=vmem)


def _tile(n, target, align=MXU_DIM):
    best = None
    for t in range(align, min(n, target) + 1, align):
        if n % t == 0:
            best = t
    assert best is not None, (n, target, align)
    return best


def _adaln(x, g, shift, scale):
    ms = jnp.mean(x * x, axis=-1, keepdims=True)
    return (x * lax.rsqrt(ms + EPS)) * (g * (1.0 + scale)) + shift


def _gelu(a):
    return 0.5 * a * (1.0 + lax.erf(a * (0.5 ** 0.5)))


def _mod_kernel(c_ref, w_ref, b_ref, o_ref):
    sc = jax.nn.silu(c_ref[...])
    o_ref[0] = jnp.dot(sc, w_ref[0], preferred_element_type=F32) + b_ref[0]


def _mod(c, w, b, tn=1024):
    nl, d, n = w.shape
    nb = c.shape[0]
    tn = _tile(n, tn)
    return pl.pallas_call(
        _mod_kernel,
        grid=(nl, n // tn),
        in_specs=[
            pl.BlockSpec((nb, d), lambda l, j: (0, 0)),
            pl.BlockSpec((1, d, tn), lambda l, j: (l, 0, j)),
            pl.BlockSpec((1, 1, tn), lambda l, j: (l, 0, j)),
        ],
        out_specs=pl.BlockSpec((1, nb, tn), lambda l, j: (l, 0, j)),
        out_shape=jax.ShapeDtypeStruct((nl, nb, n), F32),
        compiler_params=_params(("arbitrary", "arbitrary")),
        name="mod",
    )(c, w, b.reshape(nl, 1, n))


def _cast_kernel(w_ref, o_ref):
    o_ref[...] = w_ref[...].astype(o_ref.dtype)


def _cast_bf16(w, block_elems=2 * 1024 * 1024):
    nl, r, c = w.shape
    tr = _tile(r, max(MXU_DIM, block_elems // c))
    return pl.pallas_call(
        _cast_kernel,
        grid=(nl, r // tr),
        in_specs=[pl.BlockSpec((1, tr, c), lambda l, i: (l, i, 0))],
        out_specs=pl.BlockSpec((1, tr, c), lambda l, i: (l, i, 0)),
        out_shape=jax.ShapeDtypeStruct(w.shape, BF16),
        compiler_params=_params(("arbitrary", "arbitrary")),
        name="cast",
    )(w)


def _one_tile_ahead(fill, main, buf_a, buf_b):
    i = pl.program_id(0)

    @pl.when(i == 0)
    def _():
        fill(buf_a)

    @pl.when(jnp.logical_and(i > 0, i % 2 == 0))
    def _():
        main(buf_b)
        fill(buf_a)

    @pl.when(i % 2 == 1)
    def _():
        main(buf_a)
        fill(buf_b)


def _ahead(x_ref, g_ref, sh_ref, sc_ref, h_a, h_b, main, x_keep=None):
    j = pl.program_id(1)
    rows_per_step = x_ref.shape[0]

    def fill(h_dst):
        rows = pl.ds(pl.multiple_of(j * rows_per_step, rows_per_step), rows_per_step)
        x = x_ref[...]
        h_dst[rows, :] = _adaln(x, g_ref[...], sh_ref[0], sc_ref[0]).astype(BF16)
        if x_keep is not None:
            x_keep[rows, :] = x

    _one_tile_ahead(fill, main, h_a, h_b)


def _ahead_specs(d, tm, nj, n_tiles, tpb, sh_row, sc_row):
    fill_tile = lambda i: jnp.minimum(i, n_tiles - 1)
    return [
        pl.BlockSpec((tm // nj, d), lambda i, j: (fill_tile(i) * nj + j, 0)),
        pl.BlockSpec((1, d), lambda i, j: (0, 0)),
        pl.BlockSpec((1, 1, d), lambda i, j: (sh_row(fill_tile(i) // tpb), 0, 0)),
        pl.BlockSpec((1, 1, d), lambda i, j: (sc_row(fill_tile(i) // tpb), 0, 0)),
    ]


def _out_tile(i):
    return jnp.maximum(i - 1, 0)


def _col_step(i, j):
    return jnp.where(i == 0, 0, j)


def _proj_gelu_kernel(x_ref, g_ref, sh_ref, sc_ref, w_ref, o_ref, h_a, h_b):
    def main(h):
        a = jnp.dot(h[...], w_ref[...], preferred_element_type=F32)
        o_ref[...] = _gelu(a).astype(o_ref.dtype)

    _ahead(x_ref, g_ref, sh_ref, sc_ref, h_a, h_b, main)


def _proj_gelu(x2, g, modt, sh_row, sc_row, w, layer, seq, tm=1024, tn=1536):
    t, d = x2.shape
    n = w.shape[2]
    tm = _tile(seq, tm)
    tn = _tile(n, tn)
    tpb = seq // tm
    n_tiles = t // tm
    nj = n // tn
    return pl.pallas_call(
        _proj_gelu_kernel,
        grid=(n_tiles + 1, nj),
        in_specs=_ahead_specs(d, tm, nj, n_tiles, tpb, sh_row, sc_row) + [
            pl.BlockSpec((None, d, tn), lambda i, j: (layer, 0, _col_step(i, j))),
        ],
        out_specs=pl.BlockSpec((tm, tn), lambda i, j: (_out_tile(i), _col_step(i, j))),
        out_shape=jax.ShapeDtypeStruct((t, n), BF16),
        scratch_shapes=[pltpu.VMEM((tm, d), BF16), pltpu.VMEM((tm, d), BF16)],
        compiler_params=_params(("arbitrary", "arbitrary")),
        name="proj_gelu",
    )(x2, g, modt, modt, w)


def _proj_heads_kernel(x_ref, g_ref, sh_ref, sc_ref, w_ref, hg_ref, o_ref, h_a, h_b, *,
                       norm, out_scale, transpose_chunk):
    def main(h):
        a = jnp.dot(h[...], w_ref[...], preferred_element_type=F32)
        for hh in range(a.shape[1] // HEAD_DIM):
            t = a[:, hh * HEAD_DIM:(hh + 1) * HEAD_DIM]
            if norm:
                ms = jnp.mean(t * t, axis=-1, keepdims=True)
                t = t * lax.rsqrt(ms + EPS) * hg_ref[...]
            if out_scale != 1.0:
                t = t * out_scale
            if transpose_chunk:
                tt = t.T
                for cc in range(t.shape[0] // transpose_chunk):
                    cols = slice(cc * transpose_chunk, (cc + 1) * transpose_chunk)
                    o_ref[0, hh, cc] = tt[:, cols].astype(o_ref.dtype)
            else:
                o_ref[0, hh] = t.astype(o_ref.dtype)

    _ahead(x_ref, g_ref, sh_ref, sc_ref, h_a, h_b, main)


def _proj_heads(x2, g, modt, sh_row, sc_row, w, layer, col0, n, head_g, seq, *, norm,
                out_scale=1.0, transpose_chunk=0, tm=512, tn=2048):
    t, d = x2.shape
    tm = _tile(seq, tm)
    tn = _tile(n, tn)
    nb = t // seq
    nh = n // HEAD_DIM
    tpb = seq // tm
    n_tiles = t // tm
    nj = n // tn
    col0_blocks = col0 // tn
    assert col0 % tn == 0
    hpt = tn // HEAD_DIM
    kern = functools.partial(_proj_heads_kernel, norm=norm, out_scale=out_scale,
                             transpose_chunk=transpose_chunk)
    if transpose_chunk:
        cpt = tm // transpose_chunk
        out_spec = pl.BlockSpec(
            (1, hpt, cpt, HEAD_DIM, transpose_chunk),
            lambda i, j: (_out_tile(i) // tpb, _col_step(i, j), _out_tile(i) % tpb, 0, 0))
        out_shape = (nb, nh, seq // transpose_chunk, HEAD_DIM, transpose_chunk)
    else:
        out_spec = pl.BlockSpec(
            (1, hpt, tm, HEAD_DIM),
            lambda i, j: (_out_tile(i) // tpb, _col_step(i, j), _out_tile(i) % tpb, 0))
        out_shape = (nb, nh, seq, HEAD_DIM)
    return pl.pallas_call(
        kern,
        grid=(n_tiles + 1, nj),
        in_specs=_ahead_specs(d, tm, nj, n_tiles, tpb, sh_row, sc_row) + [
            pl.BlockSpec((None, d, tn), lambda i, j: (layer, 0, col0_blocks + _col_step(i, j))),
            pl.BlockSpec((1, HEAD_DIM), lambda i, j: (0, 0)),
        ],
        out_specs=out_spec,
        out_shape=jax.ShapeDtypeStruct(out_shape, BF16),
        scratch_shapes=[pltpu.VMEM((tm, d), BF16), pltpu.VMEM((tm, d), BF16)],
        compiler_params=_params(("arbitrary", "arbitrary")),
        name="proj_heads",
    )(x2, g, modt, modt, w, head_g)


def _proj_forget_kernel(x_ref, g_ref, sh_ref, sc_ref, w_ref, b_ref, o_ref, *, n_heads):
    h = _adaln(x_ref[...], g_ref[...], sh_ref[0], sc_ref[0]).astype(BF16)
    f = jnp.dot(h, w_ref[...], preferred_element_type=F32) + b_ref[...]
    o_ref[0] = jax.nn.log_sigmoid(f).T[:n_heads]


def _proj_forget(x2, g, modt, sh_row, sc_row, w_pad, b_pad, n_heads, seq, tm=1024):
    t, d = x2.shape
    tm = _tile(seq, tm)
    tpb = seq // tm
    kern = functools.partial(_proj_forget_kernel, n_heads=n_heads)
    return pl.pallas_call(
        kern,
        grid=(t // tm,),
        in_specs=[
            pl.BlockSpec((tm, d), lambda i: (i, 0)),
            pl.BlockSpec((1, d), lambda i: (0, 0)),
            pl.BlockSpec((1, 1, d), lambda i: (sh_row(i // tpb), 0, 0)),
            pl.BlockSpec((1, 1, d), lambda i: (sc_row(i // tpb), 0, 0)),
            pl.BlockSpec((d, LANES), lambda i: (0, 0)),
            pl.BlockSpec((1, LANES), lambda i: (0, 0)),
        ],
        out_specs=pl.BlockSpec((1, n_heads, tm), lambda i: (i // tpb, 0, i % tpb)),
        out_shape=jax.ShapeDtypeStruct((t // seq, n_heads, seq), F32),
        compiler_params=_params(("arbitrary",)),
        name="proj_forget",
    )(x2, g, modt, modt, w_pad, b_pad)


def _gate_outproj_kernel(u_ref, v_ref, lg_ref, lb_ref, ws_ref, bst_ref, w_ref, x_ref, gate_ref,
                         o_ref, vn_scr, p_a, p_b):
    k = pl.program_id(1)
    nk, _, tk = p_a.shape
    half = v_ref.shape[1]
    gdim = half // GMLP_GROUPS

    @pl.when(k == 0)
    def _():
        o_ref[...] = x_ref[...]

    def fill(p_dst):
        v = v_ref[...].astype(F32)
        mu = jnp.mean(v, axis=-1, keepdims=True)
        vc = v - mu
        var = jnp.mean(vc * vc, axis=-1, keepdims=True)
        vn_scr[...] = (vc * lax.rsqrt(var + EPS) * lg_ref[...] + lb_ref[...]).astype(BF16)
        row = lax.broadcasted_iota(jnp.int32, (GMLP_BLOCK, GMLP_BLOCK), 0) // CHUNK
        col = lax.broadcasted_iota(jnp.int32, (GMLP_BLOCK, GMLP_BLOCK), 1) // CHUNK
        keep = col <= row
        rows = pl.ds(pl.multiple_of(k * GMLP_BLOCK, GMLP_BLOCK), GMLP_BLOCK)
        for gi in range(GMLP_GROUPS):
            ws = jnp.where(keep, ws_ref[gi], 0.0).astype(BF16)
            cols = slice(gi * gdim, (gi + 1) * gdim)
            sv = jnp.dot(ws, vn_scr[:, cols], preferred_element_type=F32) + bst_ref[:, gi:gi + 1]
            p = (u_ref[:, cols].astype(F32) * sv).astype(BF16)
            k_slice, off = divmod(gi * gdim, tk)
            p_dst[k_slice, rows, off:off + gdim] = p

    def main(p_src):
        d = jnp.dot(p_src[k], w_ref[...], preferred_element_type=F32)
        o_ref[...] += gate_ref[0] * d

    _one_tile_ahead(fill, main, p_a, p_b)


def _gate_outproj(z, ln_g, ln_b, ws, bs_t, w, layer, x2, modt, gate_row, seq, tk=1536):
    t, n2 = z.shape
    half = n2 // 2
    d = w.shape[2]
    tk = _tile(half, tk)
    nk = half // tk
    tm = nk * GMLP_BLOCK
    assert seq % tm == 0 and tk % (half // GMLP_GROUPS) == 0
    tpb = seq // tm
    n_tiles = t // tm
    fill_block = lambda i, k: jnp.minimum(i, n_tiles - 1) * nk + k
    return pl.pallas_call(
        _gate_outproj_kernel,
        grid=(n_tiles + 1, nk),
        in_specs=[
            pl.BlockSpec((GMLP_BLOCK, half), lambda i, k: (fill_block(i, k), 0)),
            pl.BlockSpec((GMLP_BLOCK, half), lambda i, k: (fill_block(i, k), 1)),
            pl.BlockSpec((1, half), lambda i, k: (0, 0)),
            pl.BlockSpec((1, half), lambda i, k: (0, 0)),
            pl.BlockSpec(ws.shape, lambda i, k: (0, 0, 0)),
            pl.BlockSpec(bs_t.shape, lambda i, k: (0, 0)),
            pl.BlockSpec((None, tk, d), lambda i, k: (layer, _col_step(i, k), 0)),
            pl.BlockSpec((tm, d), lambda i, k: (_out_tile(i), 0)),
            pl.BlockSpec((1, 1, d), lambda i, k: (gate_row(_out_tile(i) // tpb), 0, 0)),
        ],
        out_specs=pl.BlockSpec((tm, d), lambda i, k: (_out_tile(i), 0)),
        out_shape=jax.ShapeDtypeStruct((t, d), F32),
        scratch_shapes=[
            pltpu.VMEM((GMLP_BLOCK, half), BF16),
            pltpu.VMEM((nk, tm, tk), BF16),
            pltpu.VMEM((nk, tm, tk), BF16),
        ],
        compiler_params=_params(("arbitrary", "arbitrary")),
        name="gate_outproj",
    )(z, z, ln_g, ln_b, ws, bs_t, w, x2, modt)


def _outproj_kernel(a_ref, w_ref, x_ref, gate_ref, o_ref, *, nk):
    if nk == 1:
        d = jnp.dot(a_ref[...], w_ref[...], preferred_element_type=F32)
        o_ref[...] = x_ref[...] + gate_ref[0] * d
        return

    @pl.when(pl.program_id(1) == 0)
    def _():
        o_ref[...] = x_ref[...]

    d = jnp.dot(a_ref[...], w_ref[...], preferred_element_type=F32)
    o_ref[...] += gate_ref[0] * d


def _outproj(a, w, layer, x2, modt, gate_row, seq, tm=512, tk=2048):
    t, kdim = a.shape
    d = w.shape[2]
    tm = _tile(seq, tm)
    tk = _tile(kdim, tk)
    tpb = seq // tm
    nk = kdim // tk
    return pl.pallas_call(
        functools.partial(_outproj_kernel, nk=nk),
        grid=(t // tm, nk),
        in_specs=[
            pl.BlockSpec((tm, tk), lambda i, k: (i, k)),
            pl.BlockSpec((None, tk, d), lambda i, k: (layer, k, 0)),
            pl.BlockSpec((tm, d), lambda i, k: (i, 0)),
            pl.BlockSpec((1, 1, d), lambda i, k: (gate_row(i // tpb), 0, 0)),
        ],
        out_specs=pl.BlockSpec((tm, d), lambda i, k: (i, 0)),
        out_shape=jax.ShapeDtypeStruct((t, d), F32),
        compiler_params=_params(("arbitrary", "arbitrary")),
        name="outproj",
    )(a, w, x2, modt)


def _mlp_kernel(x_ref, g_ref, sh_ref, sc_ref, gate_ref, w1_ref, w2_ref, o_ref,
                h_a, h_b, x_keep):
    @pl.when(jnp.logical_and(pl.program_id(0) > 0, pl.program_id(1) == 0))
    def _():
        o_ref[...] = x_keep[...]

    def main(h):
        a = jnp.dot(h[...], w1_ref[...], preferred_element_type=F32)
        a = jnp.square(jnp.maximum(a, 0.0)).astype(BF16)
        o_ref[...] += gate_ref[0] * jnp.dot(a, w2_ref[...], preferred_element_type=F32)

    _ahead(x_ref, g_ref, sh_ref, sc_ref, h_a, h_b, main, x_keep=x_keep)


def _mlp(x2, g, modt, sh_row, sc_row, gate_row, w1, w2, layer, seq, tm=1024, tf=1024):
    t, d = x2.shape
    dff = w1.shape[2]
    tm = _tile(seq, tm)
    tf = _tile(dff, tf)
    tpb = seq // tm
    n_tiles = t // tm
    nj = dff // tf
    return pl.pallas_call(
        _mlp_kernel,
        grid=(n_tiles + 1, nj),
        in_specs=_ahead_specs(d, tm, nj, n_tiles, tpb, sh_row, sc_row) + [
            pl.BlockSpec((1, 1, d), lambda i, j: (gate_row(_out_tile(i) // tpb), 0, 0)),
            pl.BlockSpec((None, d, tf), lambda i, j: (layer, 0, _col_step(i, j))),
            pl.BlockSpec((None, tf, d), lambda i, j: (layer, _col_step(i, j), 0)),
        ],
        out_specs=pl.BlockSpec((tm, d), lambda i, j: (_out_tile(i), 0)),
        out_shape=jax.ShapeDtypeStruct((t, d), F32),
        scratch_shapes=[pltpu.VMEM((tm, d), BF16), pltpu.VMEM((tm, d), BF16),
                        pltpu.VMEM((tm, d), F32)],
        compiler_params=_params(("arbitrary", "arbitrary")),
        name="mlp",
    )(x2, g, modt, modt, modt, w1, w2)


def _cumsum_kernel(x_ref, o_ref):
    x = x_ref[...]
    n = x.shape[1]
    lane = lax.broadcasted_iota(jnp.int32, x.shape, 1)
    d = 1
    while d < n:
        x = x + jnp.where(lane >= d, pltpu.roll(x, d, axis=1), 0.0)
        d *= 2
    o_ref[...] = x


def _cumsum(x):
    return pl.pallas_call(
        _cumsum_kernel,
        out_shape=jax.ShapeDtypeStruct(x.shape, x.dtype),
        name="cumsum",
    )(x)


def _flash_kernel(q_ref, k_ref, vt_ref, f_ref, o_ref, frep_scr, sa_scr, sb_scr, *, tq):
    seq = k_ref.shape[2]
    tk = tq
    nq = seq // tq
    n_lane_groups = tq // LANES
    nt = (((1,), (1,)), ((), ()))

    f = f_ref[0, 0]
    for c in range(nq):
        fc = jnp.broadcast_to(f[:, c * tk:(c + 1) * tk], (LANES, tk))
        frep_scr[c * tk:(c + 1) * tk, :] = fc.T * LOG2E

    ones = jnp.ones((2 * SUBLANES, tk), BF16)
    key = lax.broadcasted_iota(jnp.int32, (tk, tq), 0)
    qry = lax.broadcasted_iota(jnp.int32, (tk, tq), 1)
    visible = key <= qry

    def scores(qi, c):
        q = q_ref[0, 0, qi * tq:(qi + 1) * tq, :]
        s = lax.dot_general(k_ref[0, 0, c * tk:(c + 1) * tk, :], q, nt,
                            preferred_element_type=F32)
        bias = frep_scr[qi * tq:qi * tq + 1, :] - frep_scr[c * tk:(c + 1) * tk, :]
        s = s + jnp.concatenate([bias] * n_lane_groups, axis=1)
        s = jnp.where(visible, s, -jnp.inf) if c == qi else s
        return s, jnp.max(s, axis=0, keepdims=True)

    steps = [(qi, c) for qi in range(nq) for c in range(qi + 1)]
    bufs = (sa_scr, sb_scr)
    bufs[0][...], m_next = scores(*steps[0])
    m = acc = None
    for g, (qi, c) in enumerate(steps):
        m_chunk = m_next
        if g + 1 < len(steps):
            bufs[(g + 1) % 2][...], m_next = scores(*steps[g + 1])
        s_ref = bufs[g % 2]
        v_aug = jnp.concatenate([vt_ref[0, 0, c], ones], axis=0)
        if c == 0:
            m = m_chunk
            p = jnp.exp2((s_ref[...] - m).astype(BF16))
            acc = jnp.dot(v_aug, p, preferred_element_type=F32)
        else:
            m_new = jnp.maximum(m, m_chunk)
            p = jnp.exp2((s_ref[...] - m_new).astype(BF16))
            acc = jnp.exp2(m - m_new) * acc + jnp.dot(v_aug, p, preferred_element_type=F32)
            m = m_new
        if c == qi:
            inv_l = 1.0 / acc[HEAD_DIM:HEAD_DIM + 1]
            o_ref[0, qi * tq:(qi + 1) * tq, :] = (acc[:HEAD_DIM] * inv_l).T.astype(o_ref.dtype)


def _flash(q, k, vt, fcum, tq):
    nb, nh, seq, dh = q.shape
    nq = seq // tq
    assert vt.shape == (nb, nh, nq, dh, tq)
    return pl.pallas_call(
        functools.partial(_flash_kernel, tq=tq),
        grid=(nb, nh),
        in_specs=[
            pl.BlockSpec((1, 1, seq, dh), lambda b, h: (b, h, 0, 0)),
            pl.BlockSpec((1, 1, seq, dh), lambda b, h: (b, h, 0, 0)),
            pl.BlockSpec((1, 1, nq, dh, tq), lambda b, h: (b, h, 0, 0, 0)),
            pl.BlockSpec((1, 1, 1, seq), lambda b, h: (b, h, 0, 0)),
        ],
        out_specs=pl.BlockSpec((1, seq, dh), lambda b, h: (b, 0, h)),
        out_shape=jax.ShapeDtypeStruct((nb, seq, nh * dh), BF16),
        scratch_shapes=[
            pltpu.VMEM((seq, LANES), F32),
            pltpu.VMEM((tq, tq), F32),
            pltpu.VMEM((tq, tq), F32),
        ],
        compiler_params=_params(("arbitrary", "arbitrary")),
        name="flash",
    )(q, k, vt, fcum.reshape(nb, nh, 1, seq))


def kernel(x, c, ada_w, ada_b, norm_g, mlp_w1, mlp_w2, gmlp_w_in, gmlp_ln_g, gmlp_ln_b,
           gmlp_ws, gmlp_bs, gmlp_w_out, kv_norm_g, kv_ada_w, kv_ada_b, w_kv, k_norm_g,
           w_f, b_f, attn_wq, q_norm_g, attn_wo):
    nb, seq, d = x.shape
    depth = ada_w.shape[0]
    n_a = gmlp_w_in.shape[0]
    n_heads = w_f.shape[1]

    modt = _mod(c, ada_w, ada_b).reshape(depth * nb * N_MOD, 1, d)
    kvmod = _mod(c, kv_ada_w[None], kv_ada_b[None]).reshape(nb * 2, 1, d)

    def mod_row(layer, which):
        return lambda b: (layer * nb + b) * N_MOD + which

    w_in = _cast_bf16(gmlp_w_in)
    w_out = _cast_bf16(gmlp_w_out)
    w1 = _cast_bf16(mlp_w1)
    w2 = _cast_bf16(mlp_w2)
    wkv = _cast_bf16(w_kv[None])
    wq = _cast_bf16(attn_wq)
    wo = _cast_bf16(attn_wo)

    x2 = x.reshape(nb * seq, d)
    inv_sqrt = 1.0 / float(HEAD_DIM) ** 0.5
    k_heads = v_heads = fcum = None
    for layer in range(depth):
        g1 = norm_g[layer, 0][None]
        g2 = norm_g[layer, 1][None]
        if layer < n_a:
            a = layer
            z = _proj_gelu(x2, g1, modt, mod_row(layer, 0), mod_row(layer, 1), w_in, a, seq)
            x2 = _gate_outproj(z, gmlp_ln_g[a][None], gmlp_ln_b[a][None], gmlp_ws[a],
                               gmlp_bs[a].T, w_out, a, x2, modt, mod_row(layer, 2), seq)
        else:
            if layer == n_a:
                kv_sh = lambda b: b * 2
                kv_sc = lambda b: b * 2 + 1
                kvg = kv_norm_g[None]
                k_heads = _proj_heads(x2, kvg, kvmod, kv_sh, kv_sc, wkv, 0, 0, d,
                                      k_norm_g[None], seq, norm=True)
                v_heads = _proj_heads(x2, kvg, kvmod, kv_sh, kv_sc, wkv, 0, d, d,
                                      k_norm_g[None], seq, norm=False,
                                      transpose_chunk=FLASH_TILE)
                w_f_pad = jnp.pad(w_f, ((0, 0), (0, LANES - n_heads))).astype(BF16)
                b_f_pad = jnp.pad(b_f, (0, LANES - n_heads))[None]
                logf = _proj_forget(x2, kvg, kvmod, kv_sh, kv_sc, w_f_pad, b_f_pad,
                                    n_heads, seq)
                fcum = _cumsum(logf.reshape(nb * n_heads, seq)).reshape(nb, n_heads, seq)
            bl = layer - n_a
            q_heads = _proj_heads(x2, g1, modt, mod_row(layer, 0), mod_row(layer, 1),
                                  wq, bl, 0, d, q_norm_g[bl][None], seq,
                                  norm=True, out_scale=inv_sqrt * LOG2E)
            o = _flash(q_heads, k_heads, v_heads, fcum, tq=FLASH_TILE)
            x2 = _outproj(o.reshape(nb * seq, d), wo, bl, x2, modt, mod_row(layer, 2), seq)
        x2 = _mlp(x2, g2, modt, mod_row(layer, 3), mod_row(layer, 4), mod_row(layer, 5),
                  w1, w2, layer, seq)
    return x2.reshape(nb, seq, d)
```

```python
import functools

import jax
import jax.numpy as jnp
from jax import lax
from jax.experimental import pallas as pl
from jax.experimental.pallas import tpu as pltpu

F32 = jnp.float32
BF16 = jnp.bfloat16

EPS = 1e-6
CHUNK = 64
GMLP_BLOCK = 128
GMLP_GROUPS = 8
HEAD_DIM = 128
N_MOD = 6
LANES = 128
SUBLANES = 8
MXU_DIM = 256
LOG2E = 1.4426950408889634
FLASH_TILE = 512

MIB = 1024 * 1024
VMEM_LIMIT = 56 * MIB


def _params(semantics, vmem=VMEM_LIMIT):
    return pltpu.CompilerParams(dimension_semantics=semantics, vmem_limit_bytes=vmem)


def _tile(n, target, align=MXU_DIM):
    best = None
    for t in range(align, min(n, target) + 1, align):
        if n % t == 0:
            best = t
    assert best is not None, (n, target, align)
    return best


def _adaln(x, g, shift, scale):
    ms = jnp.mean(x * x, axis=-1, keepdims=True)
    return (x * lax.rsqrt(ms + EPS)) * (g * (1.0 + scale)) + shift


def _gelu(a):
    return 0.5 * a * (1.0 + lax.erf(a * (0.5 ** 0.5)))


def _mod_kernel(c_ref, w_ref, b_ref, o_ref):
    sc = jax.nn.silu(c_ref[...])
    o_ref[0] = jnp.dot(sc, w_ref[0], preferred_element_type=F32) + b_ref[0]


def _mod(c, w, b, tn=1024):
    nl, d, n = w.shape
    nb = c.shape[0]
    tn = _tile(n, tn)
    return pl.pallas_call(
        _mod_kernel,
        grid=(nl, n // tn),
        in_specs=[
            pl.BlockSpec((nb, d), lambda l, j: (0, 0)),
            pl.BlockSpec((1, d, tn), lambda l, j: (l, 0, j)),
            pl.BlockSpec((1, 1, tn), lambda l, j: (l, 0, j)),
        ],
        out_specs=pl.BlockSpec((1, nb, tn), lambda l, j: (l, 0, j)),
        out_shape=jax.ShapeDtypeStruct((nl, nb, n), F32),
        compiler_params=_params(("arbitrary", "arbitrary")),
        name="mod",
    )(c, w, b.reshape(nl, 1, n))


def _cast_kernel(w_ref, o_ref):
    o_ref[...] = w_ref[...].astype(o_ref.dtype)


def _cast_bf16(w, block_elems=2 * 1024 * 1024):
    nl, r, c = w.shape
    tr = _tile(r, max(MXU_DIM, block_elems // c))
    return pl.pallas_call(
        _cast_kernel,
        grid=(nl, r // tr),
        in_specs=[pl.BlockSpec((1, tr, c), lambda l, i: (l, i, 0))],
        out_specs=pl.BlockSpec((1, tr, c), lambda l, i: (l, i, 0)),
        out_shape=jax.ShapeDtypeStruct(w.shape, BF16),
        compiler_params=_params(("arbitrary", "arbitrary")),
        name="cast",
    )(w)


def _one_tile_ahead(fill, main, buf_a, buf_b):
    i = pl.program_id(0)

    @pl.when(i == 0)
    def _():
        fill(buf_a)

    @pl.when(jnp.logical_and(i > 0, i % 2 == 0))
    def _():
        main(buf_b)
        fill(buf_a)

    @pl.when(i % 2 == 1)
    def _():
        main(buf_a)
        fill(buf_b)


def _ahead(x_ref, g_ref, sh_ref, sc_ref, h_a, h_b, main, x_keep=None):
    j = pl.program_id(1)
    rows_per_step = x_ref.shape[0]

    def fill(h_dst):
        rows = pl.ds(pl.multiple_of(j * rows_per_step, rows_per_step), rows_per_step)
        x = x_ref[...]
        h_dst[rows, :] = _adaln(x, g_ref[...], sh_ref[0], sc_ref[0]).astype(BF16)
        if x_keep is not None:
            x_keep[rows, :] = x

    _one_tile_ahead(fill, main, h_a, h_b)


def _ahead_specs(d, tm, nj, n_tiles, tpb, sh_row, sc_row):
    fill_tile = lambda i: jnp.minimum(i, n_tiles - 1)
    return [
        pl.BlockSpec((tm // nj, d), lambda i, j: (fill_tile(i) * nj + j, 0)),
        pl.BlockSpec((1, d), lambda i, j: (0, 0)),
        pl.BlockSpec((1, 1, d), lambda i, j: (sh_row(fill_tile(i) // tpb), 0, 0)),
        pl.BlockSpec((1, 1, d), lambda i, j: (sc_row(fill_tile(i) // tpb), 0, 0)),
    ]


def _out_tile(i):
    return jnp.maximum(i - 1, 0)


def _col_step(i, j):
    return jnp.where(i == 0, 0, j)


def _proj_gelu_kernel(x_ref, g_ref, sh_ref, sc_ref, w_ref, o_ref, h_a, h_b):
    def main(h):
        a = jnp.dot(h[...], w_ref[...], preferred_element_type=F32)
        o_ref[...] = _gelu(a).astype(o_ref.dtype)

    _ahead(x_ref, g_ref, sh_ref, sc_ref, h_a, h_b, main)


def _proj_gelu(x2, g, modt, sh_row, sc_row, w, layer, seq, tm=1024, tn=1536):
    t, d = x2.shape
    n = w.shape[2]
    tm = _tile(seq, tm)
    tn = _tile(n, tn)
    tpb = seq // tm
    n_tiles = t // tm
    nj = n // tn
    return pl.pallas_call(
        _proj_gelu_kernel,
        grid=(n_tiles + 1, nj),
        in_specs=_ahead_specs(d, tm, nj, n_tiles, tpb, sh_row, sc_row) + [
            pl.BlockSpec((None, d, tn), lambda i, j: (layer, 0, _col_step(i, j))),
        ],
        out_specs=pl.BlockSpec((tm, tn), lambda i, j: (_out_tile(i), _col_step(i, j))),
        out_shape=jax.ShapeDtypeStruct((t, n), BF16),
        scratch_shapes=[pltpu.VMEM((tm, d), BF16), pltpu.VMEM((tm, d), BF16)],
        compiler_params=_params(("arbitrary", "arbitrary")),
        name="proj_gelu",
    )(x2, g, modt, modt, w)


def _proj_heads_kernel(x_ref, g_ref, sh_ref, sc_ref, w_ref, hg_ref, o_ref, h_a, h_b, *,
                       norm, out_scale, transpose_chunk):
    def main(h):
        a = jnp.dot(h[...], w_ref[...], preferred_element_type=F32)
        for hh in range(a.shape[1] // HEAD_DIM):
            t = a[:, hh * HEAD_DIM:(hh + 1) * HEAD_DIM]
            if norm:
                ms = jnp.mean(t * t, axis=-1, keepdims=True)
                t = t * lax.rsqrt(ms + EPS) * hg_ref[...]
            if out_scale != 1.0:
                t = t * out_scale
            if transpose_chunk:
                tt = t.T
                for cc in range(t.shape[0] // transpose_chunk):
                    cols = slice(cc * transpose_chunk, (cc + 1) * transpose_chunk)
                    o_ref[0, hh, cc] = tt[:, cols].astype(o_ref.dtype)
            else:
                o_ref[0, hh] = t.astype(o_ref.dtype)

    _ahead(x_ref, g_ref, sh_ref, sc_ref, h_a, h_b, main)


def _proj_heads(x2, g, modt, sh_row, sc_row, w, layer, col0, n, head_g, seq, *, norm,
                out_scale=1.0, transpose_chunk=0, tm=512, tn=2048):
    t, d = x2.shape
    tm = _tile(seq, tm)
    tn = _tile(n, tn)
    nb = t // seq
    nh = n // HEAD_DIM
    tpb = seq // tm
    n_tiles = t // tm
    nj = n // tn
    col0_blocks = col0 // tn
    assert col0 % tn == 0
    hpt = tn // HEAD_DIM
    kern = functools.partial(_proj_heads_kernel, norm=norm, out_scale=out_scale,
                             transpose_chunk=transpose_chunk)
    if transpose_chunk:
        cpt = tm // transpose_chunk
        out_spec = pl.BlockSpec(
            (1, hpt, cpt, HEAD_DIM, transpose_chunk),
            lambda i, j: (_out_tile(i) // tpb, _col_step(i, j), _out_tile(i) % tpb, 0, 0))
        out_shape = (nb, nh, seq // transpose_chunk, HEAD_DIM, transpose_chunk)
    else:
        out_spec = pl.BlockSpec(
            (1, hpt, tm, HEAD_DIM),
            lambda i, j: (_out_tile(i) // tpb, _col_step(i, j), _out_tile(i) % tpb, 0))
        out_shape = (nb, nh, seq, HEAD_DIM)
    return pl.pallas_call(
        kern,
        grid=(n_tiles + 1, nj),
        in_specs=_ahead_specs(d, tm, nj, n_tiles, tpb, sh_row, sc_row) + [
            pl.BlockSpec((None, d, tn), lambda i, j: (layer, 0, col0_blocks + _col_step(i, j))),
            pl.BlockSpec((1, HEAD_DIM), lambda i, j: (0, 0)),
        ],
        out_specs=out_spec,
        out_shape=jax.ShapeDtypeStruct(out_shape, BF16),
        scratch_shapes=[pltpu.VMEM((tm, d), BF16), pltpu.VMEM((tm, d), BF16)],
        compiler_params=_params(("arbitrary", "arbitrary")),
        name="proj_heads",
    )(x2, g, modt, modt, w, head_g)


def _proj_forget_kernel(x_ref, g_ref, sh_ref, sc_ref, w_ref, b_ref, o_ref, *, n_heads):
    h = _adaln(x_ref[...], g_ref[...], sh_ref[0], sc_ref[0]).astype(BF16)
    f = jnp.dot(h, w_ref[...], preferred_element_type=F32) + b_ref[...]
    o_ref[0] = jax.nn.log_sigmoid(f).T[:n_heads]


def _proj_forget(x2, g, modt, sh_row, sc_row, w_pad, b_pad, n_heads, seq, tm=1024):
    t, d = x2.shape
    tm = _tile(seq, tm)
    tpb = seq // tm
    kern = functools.partial(_proj_forget_kernel, n_heads=n_heads)
    return pl.pallas_call(
        kern,
        grid=(t // tm,),
        in_specs=[
            pl.BlockSpec((tm, d), lambda i: (i, 0)),
            pl.BlockSpec((1, d), lambda i: (0, 0)),
            pl.BlockSpec((1, 1, d), lambda i: (sh_row(i // tpb), 0, 0)),
            pl.BlockSpec((1, 1, d), lambda i: (sc_row(i // tpb), 0, 0)),
            pl.BlockSpec((d, LANES), lambda i: (0, 0)),
            pl.BlockSpec((1, LANES), lambda i: (0, 0)),
        ],
        out_specs=pl.BlockSpec((1, n_heads, tm), lambda i: (i // tpb, 0, i % tpb)),
        out_shape=jax.ShapeDtypeStruct((t // seq, n_heads, seq), F32),
        compiler_params=_params(("arbitrary",)),
        name="proj_forget",
    )(x2, g, modt, modt, w_pad, b_pad)


def _gate_outproj_kernel(u_ref, v_ref, lg_ref, lb_ref, ws_ref, bst_ref, w_ref, x_ref, gate_ref,
                         o_ref, vn_scr, p_a, p_b):
    k = pl.program_id(1)
    nk, _, tk = p_a.shape
    half = v_ref.shape[1]
    gdim = half // GMLP_GROUPS

    @pl.when(k == 0)
    def _():
        o_ref[...] = x_ref[...]

    def fill(p_dst):
        v = v_ref[...].astype(F32)
        mu = jnp.mean(v, axis=-1, keepdims=True)
        vc = v - mu
        var = jnp.mean(vc * vc, axis=-1, keepdims=True)
        vn_scr[...] = (vc * lax.rsqrt(var + EPS) * lg_ref[...] + lb_ref[...]).astype(BF16)
        row = lax.broadcasted_iota(jnp.int32, (GMLP_BLOCK, GMLP_BLOCK), 0) // CHUNK
        col = lax.broadcasted_iota(jnp.int32, (GMLP_BLOCK, GMLP_BLOCK), 1) // CHUNK
        keep = col <= row
        rows = pl.ds(pl.multiple_of(k * GMLP_BLOCK, GMLP_BLOCK), GMLP_BLOCK)
        for gi in range(GMLP_GROUPS):
            ws = jnp.where(keep, ws_ref[gi], 0.0).astype(BF16)
            cols = slice(gi * gdim, (gi + 1) * gdim)
            sv = jnp.dot(ws, vn_scr[:, cols], preferred_element_type=F32) + bst_ref[:, gi:gi + 1]
            p = (u_ref[:, cols].astype(F32) * sv).astype(BF16)
            k_slice, off = divmod(gi * gdim, tk)
            p_dst[k_slice, rows, off:off + gdim] = p

    def main(p_src):
        w_k = w_ref[pl.ds(pl.multiple_of(k * tk, tk), tk), :]
        o_ref[...] += gate_ref[0] * jnp.dot(p_src[k], w_k, preferred_element_type=F32)

    _one_tile_ahead(fill, main, p_a, p_b)


def _gate_outproj(z, ln_g, ln_b, ws, bs_t, w, layer, x2, modt, gate_row, seq, tk=3072):
    t, n2 = z.shape
    half = n2 // 2
    d = w.shape[2]
    tk = _tile(half, tk)
    nk = half // tk
    tm = nk * GMLP_BLOCK
    assert seq % tm == 0 and tk % (half // GMLP_GROUPS) == 0
    tpb = seq // tm
    n_tiles = t // tm
    fill_block = lambda i, k: jnp.minimum(i, n_tiles - 1) * nk + k
    return pl.pallas_call(
        _gate_outproj_kernel,
        grid=(n_tiles + 1, nk),
        in_specs=[
            pl.BlockSpec((GMLP_BLOCK, half), lambda i, k: (fill_block(i, k), 0)),
            pl.BlockSpec((GMLP_BLOCK, half), lambda i, k: (fill_block(i, k), 1)),
            pl.BlockSpec((1, half), lambda i, k: (0, 0)),
            pl.BlockSpec((1, half), lambda i, k: (0, 0)),
            pl.BlockSpec(ws.shape, lambda i, k: (0, 0, 0)),
            pl.BlockSpec(bs_t.shape, lambda i, k: (0, 0)),
            pl.BlockSpec((None, half, d), lambda i, k: (layer, 0, 0),
                         pipeline_mode=pl.Buffered(1)),
            pl.BlockSpec((tm, d), lambda i, k: (_out_tile(i), 0)),
            pl.BlockSpec((1, 1, d), lambda i, k: (gate_row(_out_tile(i) // tpb), 0, 0)),
        ],
        out_specs=pl.BlockSpec((tm, d), lambda i, k: (_out_tile(i), 0)),
        out_shape=jax.ShapeDtypeStruct((t, d), F32),
        scratch_shapes=[
            pltpu.VMEM((GMLP_BLOCK, half), BF16),
            pltpu.VMEM((nk, tm, tk), BF16),
            pltpu.VMEM((nk, tm, tk), BF16),
        ],
        compiler_params=_params(("arbitrary", "arbitrary")),
        name="gate_outproj",
    )(z, z, ln_g, ln_b, ws, bs_t, w, x2, modt)


def _outproj_kernel(a_ref, w_ref, x_ref, gate_ref, o_ref, *, nk):
    if nk == 1:
        d = jnp.dot(a_ref[...], w_ref[...], preferred_element_type=F32)
        o_ref[...] = x_ref[...] + gate_ref[0] * d
        return

    @pl.when(pl.program_id(1) == 0)
    def _():
        o_ref[...] = x_ref[...]

    d = jnp.dot(a_ref[...], w_ref[...], preferred_element_type=F32)
    o_ref[...] += gate_ref[0] * d


def _outproj(a, w, layer, x2, modt, gate_row, seq, tm=512, tk=2048):
    t, kdim = a.shape
    d = w.shape[2]
    tm = _tile(seq, tm)
    tk = _tile(kdim, tk)
    tpb = seq // tm
    nk = kdim // tk
    return pl.pallas_call(
        functools.partial(_outproj_kernel, nk=nk),
        grid=(t // tm, nk),
        in_specs=[
            pl.BlockSpec((tm, tk), lambda i, k: (i, k)),
            pl.BlockSpec((None, tk, d), lambda i, k: (layer, k, 0)),
            pl.BlockSpec((tm, d), lambda i, k: (i, 0)),
            pl.BlockSpec((1, 1, d), lambda i, k: (gate_row(i // tpb), 0, 0)),
        ],
        out_specs=pl.BlockSpec((tm, d), lambda i, k: (i, 0)),
        out_shape=jax.ShapeDtypeStruct((t, d), F32),
        compiler_params=_params(("arbitrary", "arbitrary")),
        name="outproj",
    )(a, w, x2, modt)


def _mlp_kernel(x_ref, g_ref, sh_ref, sc_ref, gate_ref, w1_ref, w2_ref, o_ref,
                h_a, h_b, x_keep):
    @pl.when(jnp.logical_and(pl.program_id(0) > 0, pl.program_id(1) == 0))
    def _():
        o_ref[...] = x_keep[...]

    def main(h):
        a = jnp.dot(h[...], w1_ref[...], preferred_element_type=F32)
        a = jnp.square(jnp.maximum(a, 0.0)).astype(BF16)
        o_ref[...] += gate_ref[0] * jnp.dot(a, w2_ref[...], preferred_element_type=F32)

    _ahead(x_ref, g_ref, sh_ref, sc_ref, h_a, h_b, main, x_keep=x_keep)


def _mlp(x2, g, modt, sh_row, sc_row, gate_row, w1, w2, layer, seq, tm=1024, tf=1024):
    t, d = x2.shape
    dff = w1.shape[2]
    tm = _tile(seq, tm)
    tf = _tile(dff, tf)
    tpb = seq // tm
    n_tiles = t // tm
    nj = dff // tf
    return pl.pallas_call(
        _mlp_kernel,
        grid=(n_tiles + 1, nj),
        in_specs=_ahead_specs(d, tm, nj, n_tiles, tpb, sh_row, sc_row) + [
            pl.BlockSpec((1, 1, d), lambda i, j: (gate_row(_out_tile(i) // tpb), 0, 0)),
            pl.BlockSpec((None, d, tf), lambda i, j: (layer, 0, _col_step(i, j))),
            pl.BlockSpec((None, tf, d), lambda i, j: (layer, _col_step(i, j), 0)),
        ],
        out_specs=pl.BlockSpec((tm, d), lambda i, j: (_out_tile(i), 0)),
        out_shape=jax.ShapeDtypeStruct((t, d), F32),
        scratch_shapes=[pltpu.VMEM((tm, d), BF16), pltpu.VMEM((tm, d), BF16),
                        pltpu.VMEM((tm, d), F32)],
        compiler_params=_params(("arbitrary", "arbitrary")),
        name="mlp",
    )(x2, g, modt, modt, modt, w1, w2)


def _cumsum_kernel(x_ref, o_ref):
    x = x_ref[...]
    n = x.shape[1]
    lane = lax.broadcasted_iota(jnp.int32, x.shape, 1)
    d = 1
    while d < n:
        x = x + jnp.where(lane >= d, pltpu.roll(x, d, axis=1), 0.0)
        d *= 2
    o_ref[...] = x


def _cumsum(x):
    return pl.pallas_call(
        _cumsum_kernel,
        out_shape=jax.ShapeDtypeStruct(x.shape, x.dtype),
        name="cumsum",
    )(x)


def _flash_kernel(q_ref, k_ref, vt_ref, f_ref, o_ref, frep_scr, sa_scr, sb_scr, *, tq):
    seq = k_ref.shape[2]
    tk = tq
    nq = seq // tq
    n_lane_groups = tq // LANES
    nt = (((1,), (1,)), ((), ()))

    f = f_ref[0, 0]
    for c in range(nq):
        fc = jnp.broadcast_to(f[:, c * tk:(c + 1) * tk], (LANES, tk))
        frep_scr[c * tk:(c + 1) * tk, :] = fc.T * LOG2E

    ones = jnp.ones((2 * SUBLANES, tk), BF16)
    key = lax.broadcasted_iota(jnp.int32, (tk, tq), 0)
    qry = lax.broadcasted_iota(jnp.int32, (tk, tq), 1)
    visible = key <= qry

    def scores(qi, c):
        q = q_ref[0, 0, qi * tq:(qi + 1) * tq, :]
        s = lax.dot_general(k_ref[0, 0, c * tk:(c + 1) * tk, :], q, nt,
                            preferred_element_type=F32)
        bias = frep_scr[qi * tq:qi * tq + 1, :] - frep_scr[c * tk:(c + 1) * tk, :]
        s = s + jnp.concatenate([bias] * n_lane_groups, axis=1)
        s = jnp.where(visible, s, -jnp.inf) if c == qi else s
        return s, jnp.max(s, axis=0, keepdims=True)

    steps = [(qi, c) for qi in range(nq) for c in range(qi + 1)]
    bufs = (sa_scr, sb_scr)
    bufs[0][...], m_next = scores(*steps[0])
    m = acc = None
    for g, (qi, c) in enumerate(steps):
        m_chunk = m_next
        if g + 1 < len(steps):
            bufs[(g + 1) % 2][...], m_next = scores(*steps[g + 1])
        s_ref = bufs[g % 2]
        v_aug = jnp.concatenate([vt_ref[0, 0, c], ones], axis=0)
        if c == 0:
            m = m_chunk
            p = jnp.exp2((s_ref[...] - m).astype(BF16))
            acc = jnp.dot(v_aug, p, preferred_element_type=F32)
        else:
            m_new = jnp.maximum(m, m_chunk)
            p = jnp.exp2((s_ref[...] - m_new).astype(BF16))
            acc = jnp.exp2(m - m_new) * acc + jnp.dot(v_aug, p, preferred_element_type=F32)
            m = m_new
        if c == qi:
            inv_l = 1.0 / acc[HEAD_DIM:HEAD_DIM + 1]
            o_ref[0, qi * tq:(qi + 1) * tq, :] = (acc[:HEAD_DIM] * inv_l).T.astype(o_ref.dtype)


def _flash(q, k, vt, fcum, tq):
    nb, nh, seq, dh = q.shape
    nq = seq // tq
    assert vt.shape == (nb, nh, nq, dh, tq)
    return pl.pallas_call(
        functools.partial(_flash_kernel, tq=tq),
        grid=(nb, nh),
        in_specs=[
            pl.BlockSpec((1, 1, seq, dh), lambda b, h: (b, h, 0, 0)),
            pl.BlockSpec((1, 1, seq, dh), lambda b, h: (b, h, 0, 0)),
            pl.BlockSpec((1, 1, nq, dh, tq), lambda b, h: (b, h, 0, 0, 0)),
            pl.BlockSpec((1, 1, 1, seq), lambda b, h: (b, h, 0, 0)),
        ],
        out_specs=pl.BlockSpec((1, seq, dh), lambda b, h: (b, 0, h)),
        out_shape=jax.ShapeDtypeStruct((nb, seq, nh * dh), BF16),
        scratch_shapes=[
            pltpu.VMEM((seq, LANES), F32),
            pltpu.VMEM((tq, tq), F32),
            pltpu.VMEM((tq, tq), F32),
        ],
        compiler_params=_params(("arbitrary", "arbitrary")),
        name="flash",
    )(q, k, vt, fcum.reshape(nb, nh, 1, seq))


def kernel(x, c, ada_w, ada_b, norm_g, mlp_w1, mlp_w2, gmlp_w_in, gmlp_ln_g, gmlp_ln_b,
           gmlp_ws, gmlp_bs, gmlp_w_out, kv_norm_g, kv_ada_w, kv_ada_b, w_kv, k_norm_g,
           w_f, b_f, attn_wq, q_norm_g, attn_wo):
    nb, seq, d = x.shape
    depth = ada_w.shape[0]
    n_a = gmlp_w_in.shape[0]
    n_heads = w_f.shape[1]

    modt = _mod(c, ada_w, ada_b).reshape(depth * nb * N_MOD, 1, d)
    kvmod = _mod(c, kv_ada_w[None], kv_ada_b[None]).reshape(nb * 2, 1, d)

    def mod_row(layer, which):
        return lambda b: (layer * nb + b) * N_MOD + which

    w_in = _cast_bf16(gmlp_w_in)
    w_out = _cast_bf16(gmlp_w_out)
    w1 = _cast_bf16(mlp_w1)
    w2 = _cast_bf16(mlp_w2)
    wkv = _cast_bf16(w_kv[None])
    wq = _cast_bf16(attn_wq)
    wo = _cast_bf16(attn_wo)

    x2 = x.reshape(nb * seq, d)
    inv_sqrt = 1.0 / float(HEAD_DIM) ** 0.5
    k_heads = v_heads = fcum = None
    for layer in range(depth):
        g1 = norm_g[layer, 0][None]
        g2 = norm_g[layer, 1][None]
        if layer < n_a:
            a = layer
            z = _proj_gelu(x2, g1, modt, mod_row(layer, 0), mod_row(layer, 1), w_in, a, seq)
            x2 = _gate_outproj(z, gmlp_ln_g[a][None], gmlp_ln_b[a][None], gmlp_ws[a],
                               gmlp_bs[a].T, w_out, a, x2, modt, mod_row(layer, 2), seq)
        else:
            if layer == n_a:
                kv_sh = lambda b: b * 2
                kv_sc = lambda b: b * 2 + 1
                kvg = kv_norm_g[None]
                k_heads = _proj_heads(x2, kvg, kvmod, kv_sh, kv_sc, wkv, 0, 0, d,
                                      k_norm_g[None], seq, norm=True)
                v_heads = _proj_heads(x2, kvg, kvmod, kv_sh, kv_sc, wkv, 0, d, d,
                                      k_norm_g[None], seq, norm=False,
                                      transpose_chunk=FLASH_TILE)
                w_f_pad = jnp.pad(w_f, ((0, 0), (0, LANES - n_heads))).astype(BF16)
                b_f_pad = jnp.pad(b_f, (0, LANES - n_heads))[None]
                logf = _proj_forget(x2, kvg, kvmod, kv_sh, kv_sc, w_f_pad, b_f_pad,
                                    n_heads, seq)
                fcum = _cumsum(logf.reshape(nb * n_heads, seq)).reshape(nb, n_heads, seq)
            bl = layer - n_a
            q_heads = _proj_heads(x2, g1, modt, mod_row(layer, 0), mod_row(layer, 1),
                                  wq, bl, 0, d, q_norm_g[bl][None], seq,
                                  norm=True, out_scale=inv_sqrt * LOG2E)
            o = _flash(q_heads, k_heads, v_heads, fcum, tq=FLASH_TILE)
            x2 = _outproj(o.reshape(nb * seq, d), wo, bl, x2, modt, mod_row(layer, 2), seq)
        x2 = _mlp(x2, g2, modt, mod_row(layer, 3), mod_row(layer, 4), mod_row(layer, 5),
                  w1, w2, layer, seq)
    return x2.reshape(nb, seq, d)
```

```python
import functools

import jax
import jax.numpy as jnp
from jax import lax
from jax.experimental import pallas as pl
from jax.experimental.pallas import tpu as pltpu

F32 = jnp.float32
BF16 = jnp.bfloat16

EPS = 1e-6
CHUNK = 64
GMLP_BLOCK = 128
GMLP_GROUPS = 8
HEAD_DIM = 128
N_MOD = 6
LANES = 128
SUBLANES = 8
MXU_DIM = 256
LOG2E = 1.4426950408889634
FLASH_TILE = 512

MIB = 1024 * 1024
VMEM_LIMIT = 56 * MIB


def _params(semantics, vmem=VMEM_LIMIT):
    return pltpu.CompilerParams(dimension_semantics=semantics, vmem_limit_bytes=vmem)


def _tile(n, target, align=MXU_DIM):
    best = None
    for t in range(align, min(n, target) + 1, align):
        if n % t == 0:
            best = t
    assert best is not None, (n, target, align)
    return best


def _adaln(x, g, shift, scale):
    ms = jnp.mean(x * x, axis=-1, keepdims=True)
    return (x * lax.rsqrt(ms + EPS)) * (g * (1.0 + scale)) + shift


def _gelu(a):
    return 0.5 * a * (1.0 + lax.erf(a * (0.5 ** 0.5)))


def _mod_kernel(c_ref, w_ref, b_ref, o_ref):
    sc = jax.nn.silu(c_ref[...])
    o_ref[0] = jnp.dot(sc, w_ref[0], preferred_element_type=F32) + b_ref[0]


def _mod(c, w, b, tn=1024):
    nl, d, n = w.shape
    nb = c.shape[0]
    tn = _tile(n, tn)
    return pl.pallas_call(
        _mod_kernel,
        grid=(nl, n // tn),
        in_specs=[
            pl.BlockSpec((nb, d), lambda l, j: (0, 0)),
            pl.BlockSpec((1, d, tn), lambda l, j: (l, 0, j)),
            pl.BlockSpec((1, 1, tn), lambda l, j: (l, 0, j)),
        ],
        out_specs=pl.BlockSpec((1, nb, tn), lambda l, j: (l, 0, j)),
        out_shape=jax.ShapeDtypeStruct((nl, nb, n), F32),
        compiler_params=_params(("arbitrary", "arbitrary")),
        name="mod",
    )(c, w, b.reshape(nl, 1, n))


def _cast_kernel(w_ref, o_ref):
    o_ref[...] = w_ref[...].astype(o_ref.dtype)


def _cast_bf16(w, block_elems=2 * 1024 * 1024):
    nl, r, c = w.shape
    tr = _tile(r, max(MXU_DIM, block_elems // c))
    return pl.pallas_call(
        _cast_kernel,
        grid=(nl, r // tr),
        in_specs=[pl.BlockSpec((1, tr, c), lambda l, i: (l, i, 0))],
        out_specs=pl.BlockSpec((1, tr, c), lambda l, i: (l, i, 0)),
        out_shape=jax.ShapeDtypeStruct(w.shape, BF16),
        compiler_params=_params(("arbitrary", "arbitrary")),
        name="cast",
    )(w)


def _one_tile_ahead(fill, main, buf_a, buf_b):
    i = pl.program_id(0)

    @pl.when(i == 0)
    def _():
        fill(buf_a)

    @pl.when(jnp.logical_and(i > 0, i % 2 == 0))
    def _():
        main(buf_b)
        fill(buf_a)

    @pl.when(i % 2 == 1)
    def _():
        main(buf_a)
        fill(buf_b)


def _ahead(x_ref, g_ref, sh_ref, sc_ref, h_a, h_b, main, x_keep=None):
    j = pl.program_id(1)
    rows_per_step = x_ref.shape[0]

    def fill(h_dst):
        rows = pl.ds(pl.multiple_of(j * rows_per_step, rows_per_step), rows_per_step)
        x = x_ref[...]
        h_dst[rows, :] = _adaln(x, g_ref[...], sh_ref[0], sc_ref[0]).astype(BF16)
        if x_keep is not None:
            x_keep[rows, :] = x

    _one_tile_ahead(fill, main, h_a, h_b)


def _ahead_specs(d, tm, nj, n_tiles, tpb, sh_row, sc_row):
    fill_tile = lambda i: jnp.minimum(i, n_tiles - 1)
    return [
        pl.BlockSpec((tm // nj, d), lambda i, j: (fill_tile(i) * nj + j, 0)),
        pl.BlockSpec((1, d), lambda i, j: (0, 0)),
        pl.BlockSpec((1, 1, d), lambda i, j: (sh_row(fill_tile(i) // tpb), 0, 0)),
        pl.BlockSpec((1, 1, d), lambda i, j: (sc_row(fill_tile(i) // tpb), 0, 0)),
    ]


def _out_tile(i):
    return jnp.maximum(i - 1, 0)


def _col_step(i, j):
    return jnp.where(i == 0, 0, j)


def _proj_gelu_kernel(x_ref, g_ref, sh_ref, sc_ref, w_ref, o_ref, h_a, h_b):
    def main(h):
        a = jnp.dot(h[...], w_ref[...], preferred_element_type=F32)
        o_ref[...] = _gelu(a).astype(o_ref.dtype)

    _ahead(x_ref, g_ref, sh_ref, sc_ref, h_a, h_b, main)


def _proj_gelu(x2, g, modt, sh_row, sc_row, w, layer, seq, tm=1024, tn=1536):
    t, d = x2.shape
    n = w.shape[2]
    tm = _tile(seq, tm)
    tn = _tile(n, tn)
    tpb = seq // tm
    n_tiles = t // tm
    nj = n // tn
    return pl.pallas_call(
        _proj_gelu_kernel,
        grid=(n_tiles + 1, nj),
        in_specs=_ahead_specs(d, tm, nj, n_tiles, tpb, sh_row, sc_row) + [
            pl.BlockSpec((None, d, tn), lambda i, j: (layer, 0, _col_step(i, j))),
        ],
        out_specs=pl.BlockSpec((tm, tn), lambda i, j: (_out_tile(i), _col_step(i, j))),
        out_shape=jax.ShapeDtypeStruct((t, n), BF16),
        scratch_shapes=[pltpu.VMEM((tm, d), BF16), pltpu.VMEM((tm, d), BF16)],
        compiler_params=_params(("arbitrary", "arbitrary")),
        name="proj_gelu",
    )(x2, g, modt, modt, w)


def _proj_heads_kernel(x_ref, g_ref, sh_ref, sc_ref, w_ref, hg_ref, o_ref, h_a, h_b, *,
                       norm, out_scale, transpose_chunk):
    def main(h):
        a = jnp.dot(h[...], w_ref[...], preferred_element_type=F32)
        for hh in range(a.shape[1] // HEAD_DIM):
            t = a[:, hh * HEAD_DIM:(hh + 1) * HEAD_DIM]
            if norm:
                ms = jnp.mean(t * t, axis=-1, keepdims=True)
                t = t * lax.rsqrt(ms + EPS) * hg_ref[...]
            if out_scale != 1.0:
                t = t * out_scale
            if transpose_chunk:
                tt = t.T
                for cc in range(t.shape[0] // transpose_chunk):
                    cols = slice(cc * transpose_chunk, (cc + 1) * transpose_chunk)
                    o_ref[0, hh, cc] = tt[:, cols].astype(o_ref.dtype)
            else:
                o_ref[0, hh] = t.astype(o_ref.dtype)

    _ahead(x_ref, g_ref, sh_ref, sc_ref, h_a, h_b, main)


def _proj_heads(x2, g, modt, sh_row, sc_row, w, layer, col0, n, head_g, seq, *, norm,
                out_scale=1.0, transpose_chunk=0, tm=512, tn=2048):
    t, d = x2.shape
    tm = _tile(seq, tm)
    tn = _tile(n, tn)
    nb = t // seq
    nh = n // HEAD_DIM
    tpb = seq // tm
    n_tiles = t // tm
    nj = n // tn
    col0_blocks = col0 // tn
    assert col0 % tn == 0
    hpt = tn // HEAD_DIM
    kern = functools.partial(_proj_heads_kernel, norm=norm, out_scale=out_scale,
                             transpose_chunk=transpose_chunk)
    if transpose_chunk:
        cpt = tm // transpose_chunk
        out_spec = pl.BlockSpec(
            (1, hpt, cpt, HEAD_DIM, transpose_chunk),
            lambda i, j: (_out_tile(i) // tpb, _col_step(i, j), _out_tile(i) % tpb, 0, 0))
        out_shape = (nb, nh, seq // transpose_chunk, HEAD_DIM, transpose_chunk)
    else:
        out_spec = pl.BlockSpec(
            (1, hpt, tm, HEAD_DIM),
            lambda i, j: (_out_tile(i) // tpb, _col_step(i, j), _out_tile(i) % tpb, 0))
        out_shape = (nb, nh, seq, HEAD_DIM)
    return pl.pallas_call(
        kern,
        grid=(n_tiles + 1, nj),
        in_specs=_ahead_specs(d, tm, nj, n_tiles, tpb, sh_row, sc_row) + [
            pl.BlockSpec((None, d, tn), lambda i, j: (layer, 0, col0_blocks + _col_step(i, j))),
            pl.BlockSpec((1, HEAD_DIM), lambda i, j: (0, 0)),
        ],
        out_specs=out_spec,
        out_shape=jax.ShapeDtypeStruct(out_shape, BF16),
        scratch_shapes=[pltpu.VMEM((tm, d), BF16), pltpu.VMEM((tm, d), BF16)],
        compiler_params=_params(("arbitrary", "arbitrary")),
        name="proj_heads",
    )(x2, g, modt, modt, w, head_g)


def _proj_forget_kernel(x_ref, g_ref, sh_ref, sc_ref, w_ref, b_ref, o_ref, *, n_heads):
    h = _adaln(x_ref[...], g_ref[...], sh_ref[0], sc_ref[0]).astype(BF16)
    f = jnp.dot(h, w_ref[...], preferred_element_type=F32) + b_ref[...]
    o_ref[0] = jax.nn.log_sigmoid(f).T[:n_heads]


def _proj_forget(x2, g, modt, sh_row, sc_row, w_pad, b_pad, n_heads, seq, tm=1024):
    t, d = x2.shape
    tm = _tile(seq, tm)
    tpb = seq // tm
    kern = functools.partial(_proj_forget_kernel, n_heads=n_heads)
    return pl.pallas_call(
        kern,
        grid=(t // tm,),
        in_specs=[
            pl.BlockSpec((tm, d), lambda i: (i, 0)),
            pl.BlockSpec((1, d), lambda i: (0, 0)),
            pl.BlockSpec((1, 1, d), lambda i: (sh_row(i // tpb), 0, 0)),
            pl.BlockSpec((1, 1, d), lambda i: (sc_row(i // tpb), 0, 0)),
            pl.BlockSpec((d, LANES), lambda i: (0, 0)),
            pl.BlockSpec((1, LANES), lambda i: (0, 0)),
        ],
        out_specs=pl.BlockSpec((1, n_heads, tm), lambda i: (i // tpb, 0, i % tpb)),
        out_shape=jax.ShapeDtypeStruct((t // seq, n_heads, seq), F32),
        compiler_params=_params(("arbitrary",)),
        name="proj_forget",
    )(x2, g, modt, modt, w_pad, b_pad)


def _gate_outproj_kernel(u_ref, v_ref, lg_ref, lb_ref, ws_ref, bst_ref, w_ref, x_ref, gate_ref,
                         o_ref, vn_scr, p_a, p_b):
    k = pl.program_id(1)
    nk, _, tk = p_a.shape
    half = v_ref.shape[1]
    gdim = half // GMLP_GROUPS

    @pl.when(k == 0)
    def _():
        o_ref[...] = x_ref[...]

    def fill(p_dst):
        v = v_ref[...].astype(F32)
        mu = jnp.mean(v, axis=-1, keepdims=True)
        vc = v - mu
        var = jnp.mean(vc * vc, axis=-1, keepdims=True)
        vn_scr[...] = (vc * lax.rsqrt(var + EPS) * lg_ref[...] + lb_ref[...]).astype(BF16)
        row = lax.broadcasted_iota(jnp.int32, (GMLP_BLOCK, GMLP_BLOCK), 0) // CHUNK
        col = lax.broadcasted_iota(jnp.int32, (GMLP_BLOCK, GMLP_BLOCK), 1) // CHUNK
        keep = col <= row
        rows = pl.ds(pl.multiple_of(k * GMLP_BLOCK, GMLP_BLOCK), GMLP_BLOCK)
        for gi in range(GMLP_GROUPS):
            ws = jnp.where(keep, ws_ref[gi], 0.0).astype(BF16)
            cols = slice(gi * gdim, (gi + 1) * gdim)
            sv = jnp.dot(ws, vn_scr[:, cols], preferred_element_type=F32) + bst_ref[:, gi:gi + 1]
            p = (u_ref[:, cols].astype(F32) * sv).astype(BF16)
            k_slice, off = divmod(gi * gdim, tk)
            p_dst[k_slice, rows, off:off + gdim] = p

    def main(p_src):
        d = jnp.dot(p_src[k], w_ref[...], preferred_element_type=F32)
        o_ref[...] += gate_ref[0] * d

    _one_tile_ahead(fill, main, p_a, p_b)


def _gate_outproj(z, ln_g, ln_b, ws, bs_t, w, layer, x2, modt, gate_row, seq, tk=1536):
    t, n2 = z.shape
    half = n2 // 2
    d = w.shape[2]
    tk = _tile(half, tk)
    nk = half // tk
    tm = nk * GMLP_BLOCK
    assert seq % tm == 0 and tk % (half // GMLP_GROUPS) == 0
    tpb = seq // tm
    n_tiles = t // tm
    fill_block = lambda i, k: jnp.minimum(i, n_tiles - 1) * nk + k
    return pl.pallas_call(
        _gate_outproj_kernel,
        grid=(n_tiles + 1, nk),
        in_specs=[
            pl.BlockSpec((GMLP_BLOCK, half), lambda i, k: (fill_block(i, k), 0)),
            pl.BlockSpec((GMLP_BLOCK, half), lambda i, k: (fill_block(i, k), 1)),
            pl.BlockSpec((1, half), lambda i, k: (0, 0)),
            pl.BlockSpec((1, half), lambda i, k: (0, 0)),
            pl.BlockSpec(ws.shape, lambda i, k: (0, 0, 0)),
            pl.BlockSpec(bs_t.shape, lambda i, k: (0, 0)),
            pl.BlockSpec((None, tk, d), lambda i, k: (layer, _col_step(i, k), 0)),
            pl.BlockSpec((tm, d), lambda i, k: (_out_tile(i), 0)),
            pl.BlockSpec((1, 1, d), lambda i, k: (gate_row(_out_tile(i) // tpb), 0, 0)),
        ],
        out_specs=pl.BlockSpec((tm, d), lambda i, k: (_out_tile(i), 0)),
        out_shape=jax.ShapeDtypeStruct((t, d), F32),
        scratch_shapes=[
            pltpu.VMEM((GMLP_BLOCK, half), BF16),
            pltpu.VMEM((nk, tm, tk), BF16),
            pltpu.VMEM((nk, tm, tk), BF16),
        ],
        compiler_params=_params(("arbitrary", "arbitrary")),
        name="gate_outproj",
    )(z, z, ln_g, ln_b, ws, bs_t, w, x2, modt)


def _outproj_kernel(a_ref, w_ref, x_ref, gate_ref, o_ref, *, nk):
    if nk == 1:
        d = jnp.dot(a_ref[...], w_ref[...], preferred_element_type=F32)
        o_ref[...] = x_ref[...] + gate_ref[0] * d
        return

    @pl.when(pl.program_id(1) == 0)
    def _():
        o_ref[...] = x_ref[...]

    d = jnp.dot(a_ref[...], w_ref[...], preferred_element_type=F32)
    o_ref[...] += gate_ref[0] * d


def _outproj(a, w, layer, x2, modt, gate_row, seq, tm=512, tk=2048):
    t, kdim = a.shape
    d = w.shape[2]
    tm = _tile(seq, tm)
    tk = _tile(kdim, tk)
    tpb = seq // tm
    nk = kdim // tk
    return pl.pallas_call(
        functools.partial(_outproj_kernel, nk=nk),
        grid=(t // tm, nk),
        in_specs=[
            pl.BlockSpec((tm, tk), lambda i, k: (i, k)),
            pl.BlockSpec((None, tk, d), lambda i, k: (layer, k, 0)),
            pl.BlockSpec((tm, d), lambda i, k: (i, 0)),
            pl.BlockSpec((1, 1, d), lambda i, k: (gate_row(i // tpb), 0, 0)),
        ],
        out_specs=pl.BlockSpec((tm, d), lambda i, k: (i, 0)),
        out_shape=jax.ShapeDtypeStruct((t, d), F32),
        compiler_params=_params(("arbitrary", "arbitrary")),
        name="outproj",
    )(a, w, x2, modt)


def _mlp_kernel(x_ref, g_ref, sh_ref, sc_ref, gate_ref, w1_ref, w2_ref, o_ref,
                h_a, h_b, x_keep):
    @pl.when(jnp.logical_and(pl.program_id(0) > 0, pl.program_id(1) == 0))
    def _():
        o_ref[...] = x_keep[...]

    def main(h):
        a = jnp.dot(h[...], w1_ref[...], preferred_element_type=F32)
        a = jnp.square(jnp.maximum(a, 0.0)).astype(BF16)
        o_ref[...] += gate_ref[0] * jnp.dot(a, w2_ref[...], preferred_element_type=F32)

    _ahead(x_ref, g_ref, sh_ref, sc_ref, h_a, h_b, main, x_keep=x_keep)


def _mlp(x2, g, modt, sh_row, sc_row, gate_row, w1, w2, layer, seq, tm=1024, tf=1024):
    t, d = x2.shape
    dff = w1.shape[2]
    tm = _tile(seq, tm)
    tf = _tile(dff, tf)
    tpb = seq // tm
    n_tiles = t // tm
    nj = dff // tf
    return pl.pallas_call(
        _mlp_kernel,
        grid=(n_tiles + 1, nj),
        in_specs=_ahead_specs(d, tm, nj, n_tiles, tpb, sh_row, sc_row) + [
            pl.BlockSpec((1, 1, d), lambda i, j: (gate_row(_out_tile(i) // tpb), 0, 0)),
            pl.BlockSpec((None, d, tf), lambda i, j: (layer, 0, _col_step(i, j))),
            pl.BlockSpec((None, tf, d), lambda i, j: (layer, _col_step(i, j), 0)),
        ],
        out_specs=pl.BlockSpec((tm, d), lambda i, j: (_out_tile(i), 0)),
        out_shape=jax.ShapeDtypeStruct((t, d), F32),
        scratch_shapes=[pltpu.VMEM((tm, d), BF16), pltpu.VMEM((tm, d), BF16),
                        pltpu.VMEM((tm, d), F32)],
        compiler_params=_params(("arbitrary", "arbitrary")),
        name="mlp",
    )(x2, g, modt, modt, modt, w1, w2)


def _cumsum_kernel(x_ref, o_ref):
    x = x_ref[...]
    n = x.shape[1]
    lane = lax.broadcasted_iota(jnp.int32, x.shape, 1)
    d = 1
    while d < n:
        x = x + jnp.where(lane >= d, pltpu.roll(x, d, axis=1), 0.0)
        d *= 2
    o_ref[...] = x


def _cumsum(x):
    return pl.pallas_call(
        _cumsum_kernel,
        out_shape=jax.ShapeDtypeStruct(x.shape, x.dtype),
        name="cumsum",
    )(x)


def _flash_kernel(q_ref, k_ref, vt_ref, f_ref, o_ref, frep_scr, sa_scr, sb_scr, *, tq):
    seq = k_ref.shape[2]
    tk = tq
    nq = seq // tq
    n_lane_groups = tq // LANES
    nt = (((1,), (1,)), ((), ()))

    f = f_ref[0, 0]
    for c in range(nq):
        fc = jnp.broadcast_to(f[:, c * tk:(c + 1) * tk], (LANES, tk))
        frep_scr[c * tk:(c + 1) * tk, :] = fc.T * LOG2E

    ones = jnp.ones((2 * SUBLANES, tk), BF16)
    key = lax.broadcasted_iota(jnp.int32, (tk, tq), 0)
    qry = lax.broadcasted_iota(jnp.int32, (tk, tq), 1)
    visible = key <= qry

    def scores(qi, c):
        q = q_ref[0, 0, qi * tq:(qi + 1) * tq, :]
        s = lax.dot_general(k_ref[0, 0, c * tk:(c + 1) * tk, :], q, nt,
                            preferred_element_type=F32)
        bias = frep_scr[qi * tq:qi * tq + 1, :] - frep_scr[c * tk:(c + 1) * tk, :]
        s = s + jnp.concatenate([bias] * n_lane_groups, axis=1)
        s = jnp.where(visible, s, -jnp.inf) if c == qi else s
        return s, jnp.max(s, axis=0, keepdims=True)

    steps = [(qi, c) for qi in range(nq) for c in range(qi + 1)]
    bufs = (sa_scr, sb_scr)
    bufs[0][...], m_next = scores(*steps[0])
    m = acc = None
    for g, (qi, c) in enumerate(steps):
        m_chunk = m_next
        if g + 1 < len(steps):
            bufs[(g + 1) % 2][...], m_next = scores(*steps[g + 1])
        s_ref = bufs[g % 2]
        v_aug = jnp.concatenate([vt_ref[0, 0, c], ones], axis=0)
        if c == 0:
            m = m_chunk
            p = jnp.exp2((s_ref[...] - m).astype(BF16))
            acc = jnp.dot(v_aug, p, preferred_element_type=F32)
        else:
            m_new = jnp.maximum(m, m_chunk)
            p = jnp.exp2((s_ref[...] - m_new).astype(BF16))
            acc = jnp.exp2(m - m_new) * acc + jnp.dot(v_aug, p, preferred_element_type=F32)
            m = m_new
        if c == qi:
            inv_l = 1.0 / acc[HEAD_DIM:HEAD_DIM + 1]
            o_ref[0, qi * tq:(qi + 1) * tq, :] = (acc[:HEAD_DIM] * inv_l).T.astype(o_ref.dtype)


def _flash(q, k, vt, fcum, tq):
    nb, nh, seq, dh = q.shape
    nq = seq // tq
    assert vt.shape == (nb, nh, nq, dh, tq)
    return pl.pallas_call(
        functools.partial(_flash_kernel, tq=tq),
        grid=(nb, nh),
        in_specs=[
            pl.BlockSpec((1, 1, seq, dh), lambda b, h: (b, h, 0, 0)),
            pl.BlockSpec((1, 1, seq, dh), lambda b, h: (b, h, 0, 0)),
            pl.BlockSpec((1, 1, nq, dh, tq), lambda b, h: (b, h, 0, 0, 0)),
            pl.BlockSpec((1, 1, 1, seq), lambda b, h: (b, h, 0, 0)),
        ],
        out_specs=pl.BlockSpec((1, seq, dh), lambda b, h: (b, 0, h)),
        out_shape=jax.ShapeDtypeStruct((nb, seq, nh * dh), BF16),
        scratch_shapes=[
            pltpu.VMEM((seq, LANES), F32),
            pltpu.VMEM((tq, tq), F32),
            pltpu.VMEM((tq, tq), F32),
        ],
        compiler_params=_params(("arbitrary", "arbitrary")),
        name="flash",
    )(q, k, vt, fcum.reshape(nb, nh, 1, seq))


def kernel(x, c, ada_w, ada_b, norm_g, mlp_w1, mlp_w2, gmlp_w_in, gmlp_ln_g, gmlp_ln_b,
           gmlp_ws, gmlp_bs, gmlp_w_out, kv_norm_g, kv_ada_w, kv_ada_b, w_kv, k_norm_g,
           w_f, b_f, attn_wq, q_norm_g, attn_wo):
    nb, seq, d = x.shape
    depth = ada_w.shape[0]
    n_a = gmlp_w_in.shape[0]
    n_heads = w_f.shape[1]

    modt = _mod(c, ada_w, ada_b).reshape(depth * nb * N_MOD, 1, d)
    kvmod = _mod(c, kv_ada_w[None], kv_ada_b[None]).reshape(nb * 2, 1, d)

    def mod_row(layer, which):
        return lambda b: (layer * nb + b) * N_MOD + which

    w_in = _cast_bf16(gmlp_w_in)
    w_out = _cast_bf16(gmlp_w_out)
    w1 = _cast_bf16(mlp_w1)
    w2 = _cast_bf16(mlp_w2)
    wkv = _cast_bf16(w_kv[None])
    wq = _cast_bf16(attn_wq)
    wo = _cast_bf16(attn_wo)

    x2 = x.reshape(nb * seq, d)
    inv_sqrt = 1.0 / float(HEAD_DIM) ** 0.5
    k_heads = v_heads = fcum = None
    for layer in range(depth):
        g1 = norm_g[layer, 0][None]
        g2 = norm_g[layer, 1][None]
        if layer < n_a:
            a = layer
            z = _proj_gelu(x2, g1, modt, mod_row(layer, 0), mod_row(layer, 1), w_in, a, seq)
            x2 = _gate_outproj(z, gmlp_ln_g[a][None], gmlp_ln_b[a][None], gmlp_ws[a],
                               gmlp_bs[a].T, w_out, a, x2, modt, mod_row(layer, 2), seq)
        else:
            if layer == n_a:
                kv_sh = lambda b: b * 2
                kv_sc = lambda b: b * 2 + 1
                kvg = kv_norm_g[None]
                k_heads = _proj_heads(x2, kvg, kvmod, kv_sh, kv_sc, wkv, 0, 0, d,
                                      k_norm_g[None], seq, norm=True)
                v_heads = _proj_heads(x2, kvg, kvmod, kv_sh, kv_sc, wkv, 0, d, d,
                                      k_norm_g[None], seq, norm=False,
                                      transpose_chunk=FLASH_TILE)
                w_f_pad = jnp.pad(w_f, ((0, 0), (0, LANES - n_heads))).astype(BF16)
                b_f_pad = jnp.pad(b_f, (0, LANES - n_heads))[None]
                logf = _proj_forget(x2, kvg, kvmod, kv_sh, kv_sc, w_f_pad, b_f_pad,
                                    n_heads, seq)
                fcum = _cumsum(logf.reshape(nb * n_heads, seq)).reshape(nb, n_heads, seq)
            bl = layer - n_a
            q_heads = _proj_heads(x2, g1, modt, mod_row(layer, 0), mod_row(layer, 1),
                                  wq, bl, 0, d, q_norm_g[bl][None], seq,
                                  norm=True, out_scale=inv_sqrt * LOG2E)
            o = _flash(q_heads, k_heads, v_heads, fcum, tq=FLASH_TILE)
            x2 = _outproj(o.reshape(nb * seq, d), wo, bl, x2, modt, mod_row(layer, 2), seq)
        x2 = _mlp(x2, g2, modt, mod_row(layer, 3), mod_row(layer, 4), mod_row(layer, 5),
                  w1, w2, layer, seq)
    return x2.reshape(nb, seq, d)
```

```python
import functools

import jax
import jax.numpy as jnp
from jax import lax
from jax.experimental import pallas as pl
from jax.experimental.pallas import tpu as pltpu

F32 = jnp.float32
BF16 = jnp.bfloat16

EPS = 1e-6
CHUNK = 64
GMLP_BLOCK = 128
GMLP_GROUPS = 8
HEAD_DIM = 128
N_MOD = 6
LANES = 128
SUBLANES = 8
MXU_DIM = 256
LOG2E = 1.4426950408889634
FLASH_TILE = 512

MIB = 1024 * 1024
VMEM_LIMIT = 56 * MIB


def _params(semantics, vmem=VMEM_LIMIT):
    return pltpu.CompilerParams(dimension_semantics=semantics, vmem_limit_bytes=vmem)


def _tile(n, target, align=MXU_DIM):
    best = None
    for t in range(align, min(n, target) + 1, align):
        if n % t == 0:
            best = t
    assert best is not None, (n, target, align)
    return best


def _adaln(x, g, shift, scale):
    ms = jnp.mean(x * x, axis=-1, keepdims=True)
    return (x * lax.rsqrt(ms + EPS)) * (g * (1.0 + scale)) + shift


def _gelu(a):
    return 0.5 * a * (1.0 + lax.erf(a * (0.5 ** 0.5)))


def _mod_kernel(c_ref, w_ref, b_ref, o_ref):
    sc = jax.nn.silu(c_ref[...])
    o_ref[0] = jnp.dot(sc, w_ref[0], preferred_element_type=F32) + b_ref[0]


def _mod(c, w, b, tn=1024):
    nl, d, n = w.shape
    nb = c.shape[0]
    tn = _tile(n, tn)
    return pl.pallas_call(
        _mod_kernel,
        grid=(nl, n // tn),
        in_specs=[
            pl.BlockSpec((nb, d), lambda l, j: (0, 0)),
            pl.BlockSpec((1, d, tn), lambda l, j: (l, 0, j)),
            pl.BlockSpec((1, 1, tn), lambda l, j: (l, 0, j)),
        ],
        out_specs=pl.BlockSpec((1, nb, tn), lambda l, j: (l, 0, j)),
        out_shape=jax.ShapeDtypeStruct((nl, nb, n), F32),
        compiler_params=_params(("arbitrary", "arbitrary")),
        name="mod",
    )(c, w, b.reshape(nl, 1, n))


def _cast_kernel(w_ref, o_ref):
    o_ref[...] = w_ref[...].astype(o_ref.dtype)


def _cast_bf16(w, block_elems=2 * 1024 * 1024):
    nl, r, c = w.shape
    tr = _tile(r, max(MXU_DIM, block_elems // c))
    return pl.pallas_call(
        _cast_kernel,
        grid=(nl, r // tr),
        in_specs=[pl.BlockSpec((1, tr, c), lambda l, i: (l, i, 0))],
        out_specs=pl.BlockSpec((1, tr, c), lambda l, i: (l, i, 0)),
        out_shape=jax.ShapeDtypeStruct(w.shape, BF16),
        compiler_params=_params(("arbitrary", "arbitrary")),
        name="cast",
    )(w)


def _one_tile_ahead(fill, main, buf_a, buf_b):
    i = pl.program_id(0)

    @pl.when(i == 0)
    def _():
        fill(buf_a)

    @pl.when(jnp.logical_and(i > 0, i % 2 == 0))
    def _():
        main(buf_b)
        fill(buf_a)

    @pl.when(i % 2 == 1)
    def _():
        main(buf_a)
        fill(buf_b)


def _ahead(x_ref, g_ref, sh_ref, sc_ref, h_a, h_b, main, x_keep=None):
    j = pl.program_id(1)
    rows_per_step = x_ref.shape[0]

    def fill(h_dst):
        rows = pl.ds(pl.multiple_of(j * rows_per_step, rows_per_step), rows_per_step)
        x = x_ref[...]
        h_dst[rows, :] = _adaln(x, g_ref[...], sh_ref[0], sc_ref[0]).astype(BF16)
        if x_keep is not None:
            x_keep[rows, :] = x

    _one_tile_ahead(fill, main, h_a, h_b)


def _ahead_specs(d, tm, nj, n_tiles, tpb, sh_row, sc_row):
    fill_tile = lambda i: jnp.minimum(i, n_tiles - 1)
    return [
        pl.BlockSpec((tm // nj, d), lambda i, j: (fill_tile(i) * nj + j, 0)),
        pl.BlockSpec((1, d), lambda i, j: (0, 0)),
        pl.BlockSpec((1, 1, d), lambda i, j: (sh_row(fill_tile(i) // tpb), 0, 0)),
        pl.BlockSpec((1, 1, d), lambda i, j: (sc_row(fill_tile(i) // tpb), 0, 0)),
    ]


def _out_tile(i):
    return jnp.maximum(i - 1, 0)


def _col_step(i, j):
    return jnp.where(i == 0, 0, j)


def _proj_gelu_kernel(x_ref, g_ref, sh_ref, sc_ref, w_ref, o_ref, h_a, h_b):
    def main(h):
        a = jnp.dot(h[...], w_ref[...], preferred_element_type=F32)
        o_ref[...] = _gelu(a).astype(o_ref.dtype)

    _ahead(x_ref, g_ref, sh_ref, sc_ref, h_a, h_b, main)


def _proj_gelu(x2, g, modt, sh_row, sc_row, w, layer, seq, tm=1024, tn=1536):
    t, d = x2.shape
    n = w.shape[2]
    tm = _tile(seq, tm)
    tn = _tile(n, tn)
    tpb = seq // tm
    n_tiles = t // tm
    nj = n // tn
    return pl.pallas_call(
        _proj_gelu_kernel,
        grid=(n_tiles + 1, nj),
        in_specs=_ahead_specs(d, tm, nj, n_tiles, tpb, sh_row, sc_row) + [
            pl.BlockSpec((None, d, tn), lambda i, j: (layer, 0, _col_step(i, j))),
        ],
        out_specs=pl.BlockSpec((tm, tn), lambda i, j: (_out_tile(i), _col_step(i, j))),
        out_shape=jax.ShapeDtypeStruct((t, n), BF16),
        scratch_shapes=[pltpu.VMEM((tm, d), BF16), pltpu.VMEM((tm, d), BF16)],
        compiler_params=_params(("arbitrary", "arbitrary")),
        name="proj_gelu",
    )(x2, g, modt, modt, w)


def _proj_heads_kernel(x_ref, g_ref, sh_ref, sc_ref, w_ref, hg_ref, o_ref, h_a, h_b, *,
                       norm, out_scale, transpose_chunk):
    def main(h):
        a = jnp.dot(h[...], w_ref[...], preferred_element_type=F32)
        for hh in range(a.shape[1] // HEAD_DIM):
            t = a[:, hh * HEAD_DIM:(hh + 1) * HEAD_DIM]
            if norm:
                ms = jnp.mean(t * t, axis=-1, keepdims=True)
                t = t * lax.rsqrt(ms + EPS) * hg_ref[...]
            if out_scale != 1.0:
                t = t * out_scale
            if transpose_chunk:
                tt = t.T
                for cc in range(t.shape[0] // transpose_chunk):
                    cols = slice(cc * transpose_chunk, (cc + 1) * transpose_chunk)
                    o_ref[0, hh, cc] = tt[:, cols].astype(o_ref.dtype)
            else:
                o_ref[0, hh] = t.astype(o_ref.dtype)

    _ahead(x_ref, g_ref, sh_ref, sc_ref, h_a, h_b, main)


def _proj_heads(x2, g, modt, sh_row, sc_row, w, layer, col0, n, head_g, seq, *, norm,
                out_scale=1.0, transpose_chunk=0, tm=512, tn=2048):
    t, d = x2.shape
    tm = _tile(seq, tm)
    tn = _tile(n, tn)
    nb = t // seq
    nh = n // HEAD_DIM
    tpb = seq // tm
    n_tiles = t // tm
    nj = n // tn
    col0_blocks = col0 // tn
    assert col0 % tn == 0
    hpt = tn // HEAD_DIM
    kern = functools.partial(_proj_heads_kernel, norm=norm, out_scale=out_scale,
                             transpose_chunk=transpose_chunk)
    if transpose_chunk:
        cpt = tm // transpose_chunk
        out_spec = pl.BlockSpec(
            (1, hpt, cpt, HEAD_DIM, transpose_chunk),
            lambda i, j: (_out_tile(i) // tpb, _col_step(i, j), _out_tile(i) % tpb, 0, 0))
        out_shape = (nb, nh, seq // transpose_chunk, HEAD_DIM, transpose_chunk)
    else:
        out_spec = pl.BlockSpec(
            (1, hpt, tm, HEAD_DIM),
            lambda i, j: (_out_tile(i) // tpb, _col_step(i, j), _out_tile(i) % tpb, 0))
        out_shape = (nb, nh, seq, HEAD_DIM)
    return pl.pallas_call(
        kern,
        grid=(n_tiles + 1, nj),
        in_specs=_ahead_specs(d, tm, nj, n_tiles, tpb, sh_row, sc_row) + [
            pl.BlockSpec((None, d, tn), lambda i, j: (layer, 0, col0_blocks + _col_step(i, j))),
            pl.BlockSpec((1, HEAD_DIM), lambda i, j: (0, 0)),
        ],
        out_specs=out_spec,
        out_shape=jax.ShapeDtypeStruct(out_shape, BF16),
        scratch_shapes=[pltpu.VMEM((tm, d), BF16), pltpu.VMEM((tm, d), BF16)],
        compiler_params=_params(("arbitrary", "arbitrary")),
        name="proj_heads",
    )(x2, g, modt, modt, w, head_g)


def _proj_forget_kernel(x_ref, g_ref, sh_ref, sc_ref, w_ref, b_ref, o_ref, *, n_heads):
    h = _adaln(x_ref[...], g_ref[...], sh_ref[0], sc_ref[0]).astype(BF16)
    f = jnp.dot(h, w_ref[...], preferred_element_type=F32) + b_ref[...]
    o_ref[0] = jax.nn.log_sigmoid(f).T[:n_heads]


def _proj_forget(x2, g, modt, sh_row, sc_row, w_pad, b_pad, n_heads, seq, tm=1024):
    t, d = x2.shape
    tm = _tile(seq, tm)
    tpb = seq // tm
    kern = functools.partial(_proj_forget_kernel, n_heads=n_heads)
    return pl.pallas_call(
        kern,
        grid=(t // tm,),
        in_specs=[
            pl.BlockSpec((tm, d), lambda i: (i, 0)),
            pl.BlockSpec((1, d), lambda i: (0, 0)),
            pl.BlockSpec((1, 1, d), lambda i: (sh_row(i // tpb), 0, 0)),
            pl.BlockSpec((1, 1, d), lambda i: (sc_row(i // tpb), 0, 0)),
            pl.BlockSpec((d, LANES), lambda i: (0, 0)),
            pl.BlockSpec((1, LANES), lambda i: (0, 0)),
        ],
        out_specs=pl.BlockSpec((1, n_heads, tm), lambda i: (i // tpb, 0, i % tpb)),
        out_shape=jax.ShapeDtypeStruct((t // seq, n_heads, seq), F32),
        compiler_params=_params(("arbitrary",)),
        name="proj_forget",
    )(x2, g, modt, modt, w_pad, b_pad)


def _gate_outproj_kernel(u_ref, v_ref, lg_ref, lb_ref, ws_ref, bst_ref, w_ref, x_ref, gate_ref,
                         o_ref, vn_scr, p_a, p_b):
    k = pl.program_id(1)
    nk, _, tk = p_a.shape
    half = v_ref.shape[1]
    gdim = half // GMLP_GROUPS

    @pl.when(k == 0)
    def _():
        o_ref[...] = x_ref[...]

    def layer_norm_v():
        v = v_ref[...].astype(F32)
        mu = jnp.mean(v, axis=-1, keepdims=True)
        vc = v - mu
        var = jnp.mean(vc * vc, axis=-1, keepdims=True)
        vn_scr[...] = (vc * lax.rsqrt(var + EPS) * lg_ref[...] + lb_ref[...]).astype(BF16)

    def gate_group(p_dst, gi):
        row = lax.broadcasted_iota(jnp.int32, (GMLP_BLOCK, GMLP_BLOCK), 0) // CHUNK
        col = lax.broadcasted_iota(jnp.int32, (GMLP_BLOCK, GMLP_BLOCK), 1) // CHUNK
        ws = jnp.where(col <= row, ws_ref[gi], 0.0).astype(BF16)
        cols = slice(gi * gdim, (gi + 1) * gdim)
        sv = jnp.dot(ws, vn_scr[:, cols], preferred_element_type=F32) + bst_ref[:, gi:gi + 1]
        p = (u_ref[:, cols].astype(F32) * sv).astype(BF16)
        rows = pl.ds(pl.multiple_of(k * GMLP_BLOCK, GMLP_BLOCK), GMLP_BLOCK)
        k_slice, off = divmod(gi * gdim, tk)
        p_dst[k_slice, rows, off:off + gdim] = p

    def fill(p_dst):
        layer_norm_v()
        for gi in range(GMLP_GROUPS):
            gate_group(p_dst, gi)

    def steady(p_src, p_dst):
        layer_norm_v()
        n_pieces = GMLP_GROUPS // 2
        piece = o_ref.shape[1] // n_pieces
        for pi in range(n_pieces):
            cols = slice(pi * piece, (pi + 1) * piece)
            d = jnp.dot(p_src[k], w_ref[:, cols], preferred_element_type=F32)
            o_ref[:, cols] += gate_ref[0][:, cols] * d
            gate_group(p_dst, 2 * pi)
            gate_group(p_dst, 2 * pi + 1)

    i = pl.program_id(0)

    @pl.when(i == 0)
    def _():
        fill(p_a)

    @pl.when(jnp.logical_and(i > 0, i % 2 == 0))
    def _():
        steady(p_b, p_a)

    @pl.when(i % 2 == 1)
    def _():
        steady(p_a, p_b)


def _gate_outproj(z, ln_g, ln_b, ws, bs_t, w, layer, x2, modt, gate_row, seq, tk=1536):
    t, n2 = z.shape
    half = n2 // 2
    d = w.shape[2]
    tk = _tile(half, tk)
    nk = half // tk
    tm = nk * GMLP_BLOCK
    assert seq % tm == 0 and tk % (half // GMLP_GROUPS) == 0
    tpb = seq // tm
    n_tiles = t // tm
    fill_block = lambda i, k: jnp.minimum(i, n_tiles - 1) * nk + k
    return pl.pallas_call(
        _gate_outproj_kernel,
        grid=(n_tiles + 1, nk),
        in_specs=[
            pl.BlockSpec((GMLP_BLOCK, half), lambda i, k: (fill_block(i, k), 0)),
            pl.BlockSpec((GMLP_BLOCK, half), lambda i, k: (fill_block(i, k), 1)),
            pl.BlockSpec((1, half), lambda i, k: (0, 0)),
            pl.BlockSpec((1, half), lambda i, k: (0, 0)),
            pl.BlockSpec(ws.shape, lambda i, k: (0, 0, 0)),
            pl.BlockSpec(bs_t.shape, lambda i, k: (0, 0)),
            pl.BlockSpec((None, tk, d), lambda i, k: (layer, _col_step(i, k), 0)),
            pl.BlockSpec((tm, d), lambda i, k: (_out_tile(i), 0)),
            pl.BlockSpec((1, 1, d), lambda i, k: (gate_row(_out_tile(i) // tpb), 0, 0)),
        ],
        out_specs=pl.BlockSpec((tm, d), lambda i, k: (_out_tile(i), 0)),
        out_shape=jax.ShapeDtypeStruct((t, d), F32),
        scratch_shapes=[
            pltpu.VMEM((GMLP_BLOCK, half), BF16),
            pltpu.VMEM((nk, tm, tk), BF16),
            pltpu.VMEM((nk, tm, tk), BF16),
        ],
        compiler_params=_params(("arbitrary", "arbitrary")),
        name="gate_outproj",
    )(z, z, ln_g, ln_b, ws, bs_t, w, x2, modt)


def _outproj_kernel(a_ref, w_ref, x_ref, gate_ref, o_ref, *, nk):
    if nk == 1:
        d = jnp.dot(a_ref[...], w_ref[...], preferred_element_type=F32)
        o_ref[...] = x_ref[...] + gate_ref[0] * d
        return

    @pl.when(pl.program_id(1) == 0)
    def _():
        o_ref[...] = x_ref[...]

    d = jnp.dot(a_ref[...], w_ref[...], preferred_element_type=F32)
    o_ref[...] += gate_ref[0] * d


def _outproj(a, w, layer, x2, modt, gate_row, seq, tm=512, tk=2048):
    t, kdim = a.shape
    d = w.shape[2]
    tm = _tile(seq, tm)
    tk = _tile(kdim, tk)
    tpb = seq // tm
    nk = kdim // tk
    return pl.pallas_call(
        functools.partial(_outproj_kernel, nk=nk),
        grid=(t // tm, nk),
        in_specs=[
            pl.BlockSpec((tm, tk), lambda i, k: (i, k)),
            pl.BlockSpec((None, tk, d), lambda i, k: (layer, k, 0)),
            pl.BlockSpec((tm, d), lambda i, k: (i, 0)),
            pl.BlockSpec((1, 1, d), lambda i, k: (gate_row(i // tpb), 0, 0)),
        ],
        out_specs=pl.BlockSpec((tm, d), lambda i, k: (i, 0)),
        out_shape=jax.ShapeDtypeStruct((t, d), F32),
        compiler_params=_params(("arbitrary", "arbitrary")),
        name="outproj",
    )(a, w, x2, modt)


def _mlp_kernel(x_ref, g_ref, sh_ref, sc_ref, gate_ref, w1_ref, w2_ref, o_ref,
                h_a, h_b, x_keep):
    @pl.when(jnp.logical_and(pl.program_id(0) > 0, pl.program_id(1) == 0))
    def _():
        o_ref[...] = x_keep[...]

    def main(h):
        a = jnp.dot(h[...], w1_ref[...], preferred_element_type=F32)
        a = jnp.square(jnp.maximum(a, 0.0)).astype(BF16)
        o_ref[...] += gate_ref[0] * jnp.dot(a, w2_ref[...], preferred_element_type=F32)

    _ahead(x_ref, g_ref, sh_ref, sc_ref, h_a, h_b, main, x_keep=x_keep)


def _mlp(x2, g, modt, sh_row, sc_row, gate_row, w1, w2, layer, seq, tm=1024, tf=1024):
    t, d = x2.shape
    dff = w1.shape[2]
    tm = _tile(seq, tm)
    tf = _tile(dff, tf)
    tpb = seq // tm
    n_tiles = t // tm
    nj = dff // tf
    return pl.pallas_call(
        _mlp_kernel,
        grid=(n_tiles + 1, nj),
        in_specs=_ahead_specs(d, tm, nj, n_tiles, tpb, sh_row, sc_row) + [
            pl.BlockSpec((1, 1, d), lambda i, j: (gate_row(_out_tile(i) // tpb), 0, 0)),
            pl.BlockSpec((None, d, tf), lambda i, j: (layer, 0, _col_step(i, j))),
            pl.BlockSpec((None, tf, d), lambda i, j: (layer, _col_step(i, j), 0)),
        ],
        out_specs=pl.BlockSpec((tm, d), lambda i, j: (_out_tile(i), 0)),
        out_shape=jax.ShapeDtypeStruct((t, d), F32),
        scratch_shapes=[pltpu.VMEM((tm, d), BF16), pltpu.VMEM((tm, d), BF16),
                        pltpu.VMEM((tm, d), F32)],
        compiler_params=_params(("arbitrary", "arbitrary")),
        name="mlp",
    )(x2, g, modt, modt, modt, w1, w2)


def _cumsum_kernel(x_ref, o_ref):
    x = x_ref[...]
    n = x.shape[1]
    lane = lax.broadcasted_iota(jnp.int32, x.shape, 1)
    d = 1
    while d < n:
        x = x + jnp.where(lane >= d, pltpu.roll(x, d, axis=1), 0.0)
        d *= 2
    o_ref[...] = x


def _cumsum(x):
    return pl.pallas_call(
        _cumsum_kernel,
        out_shape=jax.ShapeDtypeStruct(x.shape, x.dtype),
        name="cumsum",
    )(x)


def _flash_kernel(q_ref, k_ref, vt_ref, f_ref, o_ref, frep_scr, sa_scr, sb_scr, *, tq):
    seq = k_ref.shape[2]
    tk = tq
    nq = seq // tq
    n_lane_groups = tq // LANES
    nt = (((1,), (1,)), ((), ()))

    f = f_ref[0, 0]
    for c in range(nq):
        fc = jnp.broadcast_to(f[:, c * tk:(c + 1) * tk], (LANES, tk))
        frep_scr[c * tk:(c + 1) * tk, :] = fc.T * LOG2E

    ones = jnp.ones((2 * SUBLANES, tk), BF16)
    key = lax.broadcasted_iota(jnp.int32, (tk, tq), 0)
    qry = lax.broadcasted_iota(jnp.int32, (tk, tq), 1)
    visible = key <= qry

    def scores(qi, c):
        q = q_ref[0, 0, qi * tq:(qi + 1) * tq, :]
        s = lax.dot_general(k_ref[0, 0, c * tk:(c + 1) * tk, :], q, nt,
                            preferred_element_type=F32)
        bias = frep_scr[qi * tq:qi * tq + 1, :] - frep_scr[c * tk:(c + 1) * tk, :]
        s = s + jnp.concatenate([bias] * n_lane_groups, axis=1)
        s = jnp.where(visible, s, -jnp.inf) if c == qi else s
        return s, jnp.max(s, axis=0, keepdims=True)

    steps = [(qi, c) for qi in range(nq) for c in range(qi + 1)]
    bufs = (sa_scr, sb_scr)
    bufs[0][...], m_next = scores(*steps[0])
    m = acc = None
    for g, (qi, c) in enumerate(steps):
        m_chunk = m_next
        if g + 1 < len(steps):
            bufs[(g + 1) % 2][...], m_next = scores(*steps[g + 1])
        s_ref = bufs[g % 2]
        v_aug = jnp.concatenate([vt_ref[0, 0, c], ones], axis=0)
        if c == 0:
            m = m_chunk
            p = jnp.exp2((s_ref[...] - m).astype(BF16))
            acc = jnp.dot(v_aug, p, preferred_element_type=F32)
        else:
            m_new = jnp.maximum(m, m_chunk)
            p = jnp.exp2((s_ref[...] - m_new).astype(BF16))
            acc = jnp.exp2(m - m_new) * acc + jnp.dot(v_aug, p, preferred_element_type=F32)
            m = m_new
        if c == qi:
            inv_l = 1.0 / acc[HEAD_DIM:HEAD_DIM + 1]
            o_ref[0, qi * tq:(qi + 1) * tq, :] = (acc[:HEAD_DIM] * inv_l).T.astype(o_ref.dtype)


def _flash(q, k, vt, fcum, tq):
    nb, nh, seq, dh = q.shape
    nq = seq // tq
    assert vt.shape == (nb, nh, nq, dh, tq)
    return pl.pallas_call(
        functools.partial(_flash_kernel, tq=tq),
        grid=(nb, nh),
        in_specs=[
            pl.BlockSpec((1, 1, seq, dh), lambda b, h: (b, h, 0, 0)),
            pl.BlockSpec((1, 1, seq, dh), lambda b, h: (b, h, 0, 0)),
            pl.BlockSpec((1, 1, nq, dh, tq), lambda b, h: (b, h, 0, 0, 0)),
            pl.BlockSpec((1, 1, 1, seq), lambda b, h: (b, h, 0, 0)),
        ],
        out_specs=pl.BlockSpec((1, seq, dh), lambda b, h: (b, 0, h)),
        out_shape=jax.ShapeDtypeStruct((nb, seq, nh * dh), BF16),
        scratch_shapes=[
            pltpu.VMEM((seq, LANES), F32),
            pltpu.VMEM((tq, tq), F32),
            pltpu.VMEM((tq, tq), F32),
        ],
        compiler_params=_params(("arbitrary", "arbitrary")),
        name="flash",
    )(q, k, vt, fcum.reshape(nb, nh, 1, seq))


def kernel(x, c, ada_w, ada_b, norm_g, mlp_w1, mlp_w2, gmlp_w_in, gmlp_ln_g, gmlp_ln_b,
           gmlp_ws, gmlp_bs, gmlp_w_out, kv_norm_g, kv_ada_w, kv_ada_b, w_kv, k_norm_g,
           w_f, b_f, attn_wq, q_norm_g, attn_wo):
    nb, seq, d = x.shape
    depth = ada_w.shape[0]
    n_a = gmlp_w_in.shape[0]
    n_heads = w_f.shape[1]

    modt = _mod(c, ada_w, ada_b).reshape(depth * nb * N_MOD, 1, d)
    kvmod = _mod(c, kv_ada_w[None], kv_ada_b[None]).reshape(nb * 2, 1, d)

    def mod_row(layer, which):
        return lambda b: (layer * nb + b) * N_MOD + which

    w_in = _cast_bf16(gmlp_w_in)
    w_out = _cast_bf16(gmlp_w_out)
    w1 = _cast_bf16(mlp_w1)
    w2 = _cast_bf16(mlp_w2)
    wkv = _cast_bf16(w_kv[None])
    wq = _cast_bf16(attn_wq)
    wo = _cast_bf16(attn_wo)

    x2 = x.reshape(nb * seq, d)
    inv_sqrt = 1.0 / float(HEAD_DIM) ** 0.5
    k_heads = v_heads = fcum = None
    for layer in range(depth):
        g1 = norm_g[layer, 0][None]
        g2 = norm_g[layer, 1][None]
        if layer < n_a:
            a = layer
            z = _proj_gelu(x2, g1, modt, mod_row(layer, 0), mod_row(layer, 1), w_in, a, seq)
            x2 = _gate_outproj(z, gmlp_ln_g[a][None], gmlp_ln_b[a][None], gmlp_ws[a],
                               gmlp_bs[a].T, w_out, a, x2, modt, mod_row(layer, 2), seq)
        else:
            if layer == n_a:
                kv_sh = lambda b: b * 2
                kv_sc = lambda b: b * 2 + 1
                kvg = kv_norm_g[None]
                k_heads = _proj_heads(x2, kvg, kvmod, kv_sh, kv_sc, wkv, 0, 0, d,
                                      k_norm_g[None], seq, norm=True)
                v_heads = _proj_heads(x2, kvg, kvmod, kv_sh, kv_sc, wkv, 0, d, d,
                                      k_norm_g[None], seq, norm=False,
                                      transpose_chunk=FLASH_TILE)
                w_f_pad = jnp.pad(w_f, ((0, 0), (0, LANES - n_heads))).astype(BF16)
                b_f_pad = jnp.pad(b_f, (0, LANES - n_heads))[None]
                logf = _proj_forget(x2, kvg, kvmod, kv_sh, kv_sc, w_f_pad, b_f_pad,
                                    n_heads, seq)
                fcum = _cumsum(logf.reshape(nb * n_heads, seq)).reshape(nb, n_heads, seq)
            bl = layer - n_a
            q_heads = _proj_heads(x2, g1, modt, mod_row(layer, 0), mod_row(layer, 1),
                                  wq, bl, 0, d, q_norm_g[bl][None], seq,
                                  norm=True, out_scale=inv_sqrt * LOG2E)
            o = _flash(q_heads, k_heads, v_heads, fcum, tq=FLASH_TILE)
            x2 = _outproj(o.reshape(nb * seq, d), wo, bl, x2, modt, mod_row(layer, 2), seq)
        x2 = _mlp(x2, g2, modt, mod_row(layer, 3), mod_row(layer, 4), mod_row(layer, 5),
                  w1, w2, layer, seq)
    return x2.reshape(nb, seq, d)
```

```python
import functools

import jax
import jax.numpy as jnp
from jax import lax
from jax.experimental import pallas as pl
from jax.experimental.pallas import tpu as pltpu

F32 = jnp.float32
BF16 = jnp.bfloat16

EPS = 1e-6
CHUNK = 64
GMLP_BLOCK = 128
GMLP_GROUPS = 8
HEAD_DIM = 128
N_MOD = 6
LANES = 128
SUBLANES = 8
MXU_DIM = 256
LOG2E = 1.4426950408889634
FLASH_TILE = 512

MIB = 1024 * 1024
VMEM_LIMIT = 56 * MIB


def _params(semantics, vmem=VMEM_LIMIT):
    return pltpu.CompilerParams(dimension_semantics=semantics, vmem_limit_bytes=vmem)


def _tile(n, target, align=MXU_DIM):
    best = None
    for t in range(align, min(n, target) + 1, align):
        if n % t == 0:
            best = t
    assert best is not None, (n, target, align)
    return best


def _adaln(x, g, shift, scale):
    ms = jnp.mean(x * x, axis=-1, keepdims=True)
    return (x * lax.rsqrt(ms + EPS)) * (g * (1.0 + scale)) + shift


def _gelu(a):
    return 0.5 * a * (1.0 + lax.erf(a * (0.5 ** 0.5)))


def _mod_kernel(c_ref, w_ref, b_ref, o_ref):
    sc = jax.nn.silu(c_ref[...])
    o_ref[0] = jnp.dot(sc, w_ref[0], preferred_element_type=F32) + b_ref[0]


def _mod(c, w, b, tn=1024):
    nl, d, n = w.shape
    nb = c.shape[0]
    tn = _tile(n, tn)
    return pl.pallas_call(
        _mod_kernel,
        grid=(nl, n // tn),
        in_specs=[
            pl.BlockSpec((nb, d), lambda l, j: (0, 0)),
            pl.BlockSpec((1, d, tn), lambda l, j: (l, 0, j)),
            pl.BlockSpec((1, 1, tn), lambda l, j: (l, 0, j)),
        ],
        out_specs=pl.BlockSpec((1, nb, tn), lambda l, j: (l, 0, j)),
        out_shape=jax.ShapeDtypeStruct((nl, nb, n), F32),
        compiler_params=_params(("arbitrary", "arbitrary")),
        name="mod",
    )(c, w, b.reshape(nl, 1, n))


def _cast_kernel(w_ref, o_ref):
    o_ref[...] = w_ref[...].astype(o_ref.dtype)


def _cast_bf16(w, block_elems=2 * 1024 * 1024):
    nl, r, c = w.shape
    tr = _tile(r, max(MXU_DIM, block_elems // c))
    return pl.pallas_call(
        _cast_kernel,
        grid=(nl, r // tr),
        in_specs=[pl.BlockSpec((1, tr, c), lambda l, i: (l, i, 0))],
        out_specs=pl.BlockSpec((1, tr, c), lambda l, i: (l, i, 0)),
        out_shape=jax.ShapeDtypeStruct(w.shape, BF16),
        compiler_params=_params(("arbitrary", "arbitrary")),
        name="cast",
    )(w)


def _one_tile_ahead(fill, main, buf_a, buf_b):
    i = pl.program_id(0)

    @pl.when(i == 0)
    def _():
        fill(buf_a)

    @pl.when(jnp.logical_and(i > 0, i % 2 == 0))
    def _():
        main(buf_b)
        fill(buf_a)

    @pl.when(i % 2 == 1)
    def _():
        main(buf_a)
        fill(buf_b)


def _ahead(x_ref, g_ref, sh_ref, sc_ref, h_a, h_b, main, x_keep=None):
    j = pl.program_id(1)
    rows_per_step = x_ref.shape[0]

    def fill(h_dst):
        rows = pl.ds(pl.multiple_of(j * rows_per_step, rows_per_step), rows_per_step)
        x = x_ref[...]
        h_dst[rows, :] = _adaln(x, g_ref[...], sh_ref[0], sc_ref[0]).astype(BF16)
        if x_keep is not None:
            x_keep[rows, :] = x

    _one_tile_ahead(fill, main, h_a, h_b)


def _ahead_specs(d, tm, nj, n_tiles, tpb, sh_row, sc_row):
    fill_tile = lambda i: jnp.minimum(i, n_tiles - 1)
    return [
        pl.BlockSpec((tm // nj, d), lambda i, j: (fill_tile(i) * nj + j, 0)),
        pl.BlockSpec((1, d), lambda i, j: (0, 0)),
        pl.BlockSpec((1, 1, d), lambda i, j: (sh_row(fill_tile(i) // tpb), 0, 0)),
        pl.BlockSpec((1, 1, d), lambda i, j: (sc_row(fill_tile(i) // tpb), 0, 0)),
    ]


def _out_tile(i):
    return jnp.maximum(i - 1, 0)


def _col_step(i, j):
    return jnp.where(i == 0, 0, j)


def _proj_gelu_kernel(x_ref, g_ref, sh_ref, sc_ref, w_ref, o_ref, h_a, h_b):
    def main(h):
        a = jnp.dot(h[...], w_ref[...], preferred_element_type=F32)
        o_ref[...] = _gelu(a).astype(o_ref.dtype)

    _ahead(x_ref, g_ref, sh_ref, sc_ref, h_a, h_b, main)


def _proj_gelu(x2, g, modt, sh_row, sc_row, w, layer, seq, tm=1024, tn=1536):
    t, d = x2.shape
    n = w.shape[2]
    tm = _tile(seq, tm)
    tn = _tile(n, tn)
    tpb = seq // tm
    n_tiles = t // tm
    nj = n // tn
    return pl.pallas_call(
        _proj_gelu_kernel,
        grid=(n_tiles + 1, nj),
        in_specs=_ahead_specs(d, tm, nj, n_tiles, tpb, sh_row, sc_row) + [
            pl.BlockSpec((None, d, tn), lambda i, j: (layer, 0, _col_step(i, j))),
        ],
        out_specs=pl.BlockSpec((tm, tn), lambda i, j: (_out_tile(i), _col_step(i, j))),
        out_shape=jax.ShapeDtypeStruct((t, n), BF16),
        scratch_shapes=[pltpu.VMEM((tm, d), BF16), pltpu.VMEM((tm, d), BF16)],
        compiler_params=_params(("arbitrary", "arbitrary")),
        name="proj_gelu",
    )(x2, g, modt, modt, w)


def _proj_heads_kernel(x_ref, g_ref, sh_ref, sc_ref, w_ref, hg_ref, o_ref, h_a, h_b, *,
                       norm, out_scale, transpose_chunk):
    def main(h):
        a = jnp.dot(h[...], w_ref[...], preferred_element_type=F32)
        for hh in range(a.shape[1] // HEAD_DIM):
            t = a[:, hh * HEAD_DIM:(hh + 1) * HEAD_DIM]
            if norm:
                ms = jnp.mean(t * t, axis=-1, keepdims=True)
                t = t * lax.rsqrt(ms + EPS) * hg_ref[...]
            if out_scale != 1.0:
                t = t * out_scale
            if transpose_chunk:
                tt = t.T
                for cc in range(t.shape[0] // transpose_chunk):
                    cols = slice(cc * transpose_chunk, (cc + 1) * transpose_chunk)
                    o_ref[0, hh, cc] = tt[:, cols].astype(o_ref.dtype)
            else:
                o_ref[0, hh] = t.astype(o_ref.dtype)

    _ahead(x_ref, g_ref, sh_ref, sc_ref, h_a, h_b, main)


def _proj_heads(x2, g, modt, sh_row, sc_row, w, layer, col0, n, head_g, seq, *, norm,
                out_scale=1.0, transpose_chunk=0, tm=512, tn=2048):
    t, d = x2.shape
    tm = _tile(seq, tm)
    tn = _tile(n, tn)
    nb = t // seq
    nh = n // HEAD_DIM
    tpb = seq // tm
    n_tiles = t // tm
    nj = n // tn
    col0_blocks = col0 // tn
    assert col0 % tn == 0
    hpt = tn // HEAD_DIM
    kern = functools.partial(_proj_heads_kernel, norm=norm, out_scale=out_scale,
                             transpose_chunk=transpose_chunk)
    if transpose_chunk:
        cpt = tm // transpose_chunk
        out_spec = pl.BlockSpec(
            (1, hpt, cpt, HEAD_DIM, transpose_chunk),
            lambda i, j: (_out_tile(i) // tpb, _col_step(i, j), _out_tile(i) % tpb, 0, 0))
        out_shape = (nb, nh, seq // transpose_chunk, HEAD_DIM, transpose_chunk)
    else:
        out_spec = pl.BlockSpec(
            (1, hpt, tm, HEAD_DIM),
            lambda i, j: (_out_tile(i) // tpb, _col_step(i, j), _out_tile(i) % tpb, 0))
        out_shape = (nb, nh, seq, HEAD_DIM)
    return pl.pallas_call(
        kern,
        grid=(n_tiles + 1, nj),
        in_specs=_ahead_specs(d, tm, nj, n_tiles, tpb, sh_row, sc_row) + [
            pl.BlockSpec((None, d, tn), lambda i, j: (layer, 0, col0_blocks + _col_step(i, j))),
            pl.BlockSpec((1, HEAD_DIM), lambda i, j: (0, 0)),
        ],
        out_specs=out_spec,
        out_shape=jax.ShapeDtypeStruct(out_shape, BF16),
        scratch_shapes=[pltpu.VMEM((tm, d), BF16), pltpu.VMEM((tm, d), BF16)],
        compiler_params=_params(("arbitrary", "arbitrary")),
        name="proj_heads",
    )(x2, g, modt, modt, w, head_g)


def _proj_kvf_kernel(x_ref, g_ref, sh_ref, sc_ref, w_ref, hg_ref, wf_ref, bf_ref,
                     k_ref, v_ref, f_ref, h_a, h_b, *, transpose_chunk, n_heads):
    j = pl.program_id(1)

    def main(h):
        @pl.when(j == 0)
        def _():
            a = jnp.dot(h[...], w_ref[...], preferred_element_type=F32)
            for hh in range(a.shape[1] // HEAD_DIM):
                t = a[:, hh * HEAD_DIM:(hh + 1) * HEAD_DIM]
                ms = jnp.mean(t * t, axis=-1, keepdims=True)
                k_ref[0, hh] = (t * lax.rsqrt(ms + EPS) * hg_ref[...]).astype(k_ref.dtype)

        @pl.when(j == 1)
        def _():
            a = jnp.dot(h[...], w_ref[...], preferred_element_type=F32)
            for hh in range(a.shape[1] // HEAD_DIM):
                tt = a[:, hh * HEAD_DIM:(hh + 1) * HEAD_DIM].T
                for cc in range(tt.shape[1] // transpose_chunk):
                    cols = slice(cc * transpose_chunk, (cc + 1) * transpose_chunk)
                    v_ref[0, hh, cc] = tt[:, cols].astype(v_ref.dtype)
            f = jnp.dot(h[...], wf_ref[...], preferred_element_type=F32) + bf_ref[...]
            f_ref[0] = jax.nn.log_sigmoid(f).T[:n_heads]

    _ahead(x_ref, g_ref, sh_ref, sc_ref, h_a, h_b, main)


def _proj_kvf(x2, g, modt, sh_row, sc_row, w, head_g, w_f_pad, b_f_pad, n_heads, seq,
              transpose_chunk, tm=512):
    t, d = x2.shape
    n = w.shape[2] // 2
    tm = _tile(seq, tm)
    nb = t // seq
    nh = n // HEAD_DIM
    tpb = seq // tm
    n_tiles = t // tm
    nj = 2
    cpt = tm // transpose_chunk
    tile_pos = lambda i: (_out_tile(i) // tpb, _out_tile(i) % tpb)
    kern = functools.partial(_proj_kvf_kernel, transpose_chunk=transpose_chunk, n_heads=n_heads)
    return pl.pallas_call(
        kern,
        grid=(n_tiles + 1, nj),
        in_specs=_ahead_specs(d, tm, nj, n_tiles, tpb, sh_row, sc_row) + [
            pl.BlockSpec((None, d, n), lambda i, j: (0, 0, _col_step(i, j))),
            pl.BlockSpec((1, HEAD_DIM), lambda i, j: (0, 0)),
            pl.BlockSpec((d, LANES), lambda i, j: (0, 0)),
            pl.BlockSpec((1, LANES), lambda i, j: (0, 0)),
        ],
        out_specs=[
            pl.BlockSpec((1, nh, tm, HEAD_DIM),
                         lambda i, j: (tile_pos(i)[0], 0, tile_pos(i)[1], 0)),
            pl.BlockSpec((1, nh, cpt, HEAD_DIM, transpose_chunk),
                         lambda i, j: (tile_pos(i)[0], 0, tile_pos(i)[1], 0, 0)),
            pl.BlockSpec((1, n_heads, tm), lambda i, j: (tile_pos(i)[0], 0, tile_pos(i)[1])),
        ],
        out_shape=[
            jax.ShapeDtypeStruct((nb, nh, seq, HEAD_DIM), BF16),
            jax.ShapeDtypeStruct((nb, nh, seq // transpose_chunk, HEAD_DIM, transpose_chunk), BF16),
            jax.ShapeDtypeStruct((nb, n_heads, seq), F32),
        ],
        scratch_shapes=[pltpu.VMEM((tm, d), BF16), pltpu.VMEM((tm, d), BF16)],
        compiler_params=_params(("arbitrary", "arbitrary")),
        name="proj_kvf",
    )(x2, g, modt, modt, w, head_g, w_f_pad, b_f_pad)


def _gate_outproj_kernel(u_ref, v_ref, lg_ref, lb_ref, ws_ref, bst_ref, w_ref, x_ref, gate_ref,
                         o_ref, vn_scr, p_a, p_b):
    k = pl.program_id(1)
    nk, _, tk = p_a.shape
    half = v_ref.shape[1]
    gdim = half // GMLP_GROUPS

    @pl.when(k == 0)
    def _():
        o_ref[...] = x_ref[...]

    def layer_norm_v():
        v = v_ref[...].astype(F32)
        mu = jnp.mean(v, axis=-1, keepdims=True)
        vc = v - mu
        var = jnp.mean(vc * vc, axis=-1, keepdims=True)
        vn_scr[...] = (vc * lax.rsqrt(var + EPS) * lg_ref[...] + lb_ref[...]).astype(BF16)

    def gate_group(p_dst, gi):
        row = lax.broadcasted_iota(jnp.int32, (GMLP_BLOCK, GMLP_BLOCK), 0) // CHUNK
        col = lax.broadcasted_iota(jnp.int32, (GMLP_BLOCK, GMLP_BLOCK), 1) // CHUNK
        ws = jnp.where(col <= row, ws_ref[gi], 0.0).astype(BF16)
        cols = slice(gi * gdim, (gi + 1) * gdim)
        sv = jnp.dot(ws, vn_scr[:, cols], preferred_element_type=F32) + bst_ref[:, gi:gi + 1]
        p = (u_ref[:, cols].astype(F32) * sv).astype(BF16)
        rows = pl.ds(pl.multiple_of(k * GMLP_BLOCK, GMLP_BLOCK), GMLP_BLOCK)
        k_slice, off = divmod(gi * gdim, tk)
        p_dst[k_slice, rows, off:off + gdim] = p

    def fill(p_dst):
        layer_norm_v()
        for gi in range(GMLP_GROUPS):
            gate_group(p_dst, gi)

    def steady(p_src, p_dst):
        layer_norm_v()
        n_pieces = GMLP_GROUPS // 2
        piece = o_ref.shape[1] // n_pieces
        for pi in range(n_pieces):
            cols = slice(pi * piece, (pi + 1) * piece)
            d = jnp.dot(p_src[k], w_ref[:, cols], preferred_element_type=F32)
            o_ref[:, cols] += gate_ref[0][:, cols] * d
            gate_group(p_dst, 2 * pi)
            gate_group(p_dst, 2 * pi + 1)

    i = pl.program_id(0)

    @pl.when(i == 0)
    def _():
        fill(p_a)

    @pl.when(jnp.logical_and(i > 0, i % 2 == 0))
    def _():
        steady(p_b, p_a)

    @pl.when(i % 2 == 1)
    def _():
        steady(p_a, p_b)


def _gate_outproj(z, ln_g, ln_b, ws, bs_t, w, layer, x2, modt, gate_row, seq, tk=1536):
    t, n2 = z.shape
    half = n2 // 2
    d = w.shape[2]
    tk = _tile(half, tk)
    nk = half // tk
    tm = nk * GMLP_BLOCK
    assert seq % tm == 0 and tk % (half // GMLP_GROUPS) == 0
    tpb = seq // tm
    n_tiles = t // tm
    fill_block = lambda i, k: jnp.minimum(i, n_tiles - 1) * nk + k
    return pl.pallas_call(
        _gate_outproj_kernel,
        grid=(n_tiles + 1, nk),
        in_specs=[
            pl.BlockSpec((GMLP_BLOCK, half), lambda i, k: (fill_block(i, k), 0)),
            pl.BlockSpec((GMLP_BLOCK, half), lambda i, k: (fill_block(i, k), 1)),
            pl.BlockSpec((1, half), lambda i, k: (0, 0)),
            pl.BlockSpec((1, half), lambda i, k: (0, 0)),
            pl.BlockSpec(ws.shape, lambda i, k: (0, 0, 0)),
            pl.BlockSpec(bs_t.shape, lambda i, k: (0, 0)),
            pl.BlockSpec((None, tk, d), lambda i, k: (layer, _col_step(i, k), 0)),
            pl.BlockSpec((tm, d), lambda i, k: (_out_tile(i), 0)),
            pl.BlockSpec((1, 1, d), lambda i, k: (gate_row(_out_tile(i) // tpb), 0, 0)),
        ],
        out_specs=pl.BlockSpec((tm, d), lambda i, k: (_out_tile(i), 0)),
        out_shape=jax.ShapeDtypeStruct((t, d), F32),
        scratch_shapes=[
            pltpu.VMEM((GMLP_BLOCK, half), BF16),
            pltpu.VMEM((nk, tm, tk), BF16),
            pltpu.VMEM((nk, tm, tk), BF16),
        ],
        compiler_params=_params(("arbitrary", "arbitrary")),
        name="gate_outproj",
    )(z, z, ln_g, ln_b, ws, bs_t, w, x2, modt)


def _outproj_kernel(a_ref, w_ref, x_ref, gate_ref, o_ref, *, nk):
    if nk == 1:
        d = jnp.dot(a_ref[...], w_ref[...], preferred_element_type=F32)
        o_ref[...] = x_ref[...] + gate_ref[0] * d
        return

    @pl.when(pl.program_id(1) == 0)
    def _():
        o_ref[...] = x_ref[...]

    d = jnp.dot(a_ref[...], w_ref[...], preferred_element_type=F32)
    o_ref[...] += gate_ref[0] * d


def _outproj(a, w, layer, x2, modt, gate_row, seq, tm=512, tk=2048):
    t, kdim = a.shape
    d = w.shape[2]
    tm = _tile(seq, tm)
    tk = _tile(kdim, tk)
    tpb = seq // tm
    nk = kdim // tk
    return pl.pallas_call(
        functools.partial(_outproj_kernel, nk=nk),
        grid=(t // tm, nk),
        in_specs=[
            pl.BlockSpec((tm, tk), lambda i, k: (i, k)),
            pl.BlockSpec((None, tk, d), lambda i, k: (layer, k, 0)),
            pl.BlockSpec((tm, d), lambda i, k: (i, 0)),
            pl.BlockSpec((1, 1, d), lambda i, k: (gate_row(i // tpb), 0, 0)),
        ],
        out_specs=pl.BlockSpec((tm, d), lambda i, k: (i, 0)),
        out_shape=jax.ShapeDtypeStruct((t, d), F32),
        compiler_params=_params(("arbitrary", "arbitrary")),
        name="outproj",
    )(a, w, x2, modt)


def _mlp_kernel(x_ref, g_ref, sh_ref, sc_ref, gate_ref, w1_ref, w2_ref, o_ref,
                h_a, h_b, x_keep):
    @pl.when(jnp.logical_and(pl.program_id(0) > 0, pl.program_id(1) == 0))
    def _():
        o_ref[...] = x_keep[...]

    def main(h):
        a = jnp.dot(h[...], w1_ref[...], preferred_element_type=F32)
        a = jnp.square(jnp.maximum(a, 0.0)).astype(BF16)
        o_ref[...] += gate_ref[0] * jnp.dot(a, w2_ref[...], preferred_element_type=F32)

    _ahead(x_ref, g_ref, sh_ref, sc_ref, h_a, h_b, main, x_keep=x_keep)


def _mlp(x2, g, modt, sh_row, sc_row, gate_row, w1, w2, layer, seq, tm=1024, tf=1024):
    t, d = x2.shape
    dff = w1.shape[2]
    tm = _tile(seq, tm)
    tf = _tile(dff, tf)
    tpb = seq // tm
    n_tiles = t // tm
    nj = dff // tf
    return pl.pallas_call(
        _mlp_kernel,
        grid=(n_tiles + 1, nj),
        in_specs=_ahead_specs(d, tm, nj, n_tiles, tpb, sh_row, sc_row) + [
            pl.BlockSpec((1, 1, d), lambda i, j: (gate_row(_out_tile(i) // tpb), 0, 0)),
            pl.BlockSpec((None, d, tf), lambda i, j: (layer, 0, _col_step(i, j))),
            pl.BlockSpec((None, tf, d), lambda i, j: (layer, _col_step(i, j), 0)),
        ],
        out_specs=pl.BlockSpec((tm, d), lambda i, j: (_out_tile(i), 0)),
        out_shape=jax.ShapeDtypeStruct((t, d), F32),
        scratch_shapes=[pltpu.VMEM((tm, d), BF16), pltpu.VMEM((tm, d), BF16),
                        pltpu.VMEM((tm, d), F32)],
        compiler_params=_params(("arbitrary", "arbitrary")),
        name="mlp",
    )(x2, g, modt, modt, modt, w1, w2)


def _cumsum_kernel(x_ref, o_ref):
    x = x_ref[...]
    n = x.shape[1]
    lane = lax.broadcasted_iota(jnp.int32, x.shape, 1)
    d = 1
    while d < n:
        x = x + jnp.where(lane >= d, pltpu.roll(x, d, axis=1), 0.0)
        d *= 2
    o_ref[...] = x


def _cumsum(x):
    return pl.pallas_call(
        _cumsum_kernel,
        out_shape=jax.ShapeDtypeStruct(x.shape, x.dtype),
        name="cumsum",
    )(x)


def _flash_kernel(q_ref, k_ref, vt_ref, f_ref, o_ref, frep_scr, sa_scr, sb_scr, *, tq):
    seq = k_ref.shape[2]
    tk = tq
    nq = seq // tq
    n_lane_groups = tq // LANES
    nt = (((1,), (1,)), ((), ()))

    f = f_ref[0, 0]
    for c in range(nq):
        fc = jnp.broadcast_to(f[:, c * tk:(c + 1) * tk], (LANES, tk))
        frep_scr[c * tk:(c + 1) * tk, :] = fc.T * LOG2E

    ones = jnp.ones((2 * SUBLANES, tk), BF16)
    key = lax.broadcasted_iota(jnp.int32, (tk, tq), 0)
    qry = lax.broadcasted_iota(jnp.int32, (tk, tq), 1)
    visible = key <= qry

    def scores(qi, c):
        q = q_ref[0, 0, qi * tq:(qi + 1) * tq, :]
        s = lax.dot_general(k_ref[0, 0, c * tk:(c + 1) * tk, :], q, nt,
                            preferred_element_type=F32)
        bias = frep_scr[qi * tq:qi * tq + 1, :] - frep_scr[c * tk:(c + 1) * tk, :]
        s = s + jnp.concatenate([bias] * n_lane_groups, axis=1)
        s = jnp.where(visible, s, -jnp.inf) if c == qi else s
        return s, jnp.max(s, axis=0, keepdims=True)

    steps = [(qi, c) for qi in range(nq) for c in range(qi + 1)]
    bufs = (sa_scr, sb_scr)
    bufs[0][...], m_next = scores(*steps[0])
    m = acc = None
    for g, (qi, c) in enumerate(steps):
        m_chunk = m_next
        if g + 1 < len(steps):
            bufs[(g + 1) % 2][...], m_next = scores(*steps[g + 1])
        s_ref = bufs[g % 2]
        v_aug = jnp.concatenate([vt_ref[0, 0, c], ones], axis=0)
        if c == 0:
            m = m_chunk
            p = jnp.exp2((s_ref[...] - m).astype(BF16))
            acc = jnp.dot(v_aug, p, preferred_element_type=F32)
        else:
            m_new = jnp.maximum(m, m_chunk)
            p = jnp.exp2((s_ref[...] - m_new).astype(BF16))
            acc = jnp.exp2(m - m_new) * acc + jnp.dot(v_aug, p, preferred_element_type=F32)
            m = m_new
        if c == qi:
            inv_l = 1.0 / acc[HEAD_DIM:HEAD_DIM + 1]
            o_ref[0, qi * tq:(qi + 1) * tq, :] = (acc[:HEAD_DIM] * inv_l).T.astype(o_ref.dtype)


def _flash(q, k, vt, fcum, tq):
    nb, nh, seq, dh = q.shape
    nq = seq // tq
    assert vt.shape == (nb, nh, nq, dh, tq)
    return pl.pallas_call(
        functools.partial(_flash_kernel, tq=tq),
        grid=(nb, nh),
        in_specs=[
            pl.BlockSpec((1, 1, seq, dh), lambda b, h: (b, h, 0, 0)),
            pl.BlockSpec((1, 1, seq, dh), lambda b, h: (b, h, 0, 0)),
            pl.BlockSpec((1, 1, nq, dh, tq), lambda b, h: (b, h, 0, 0, 0)),
            pl.BlockSpec((1, 1, 1, seq), lambda b, h: (b, h, 0, 0)),
        ],
        out_specs=pl.BlockSpec((1, seq, dh), lambda b, h: (b, 0, h)),
        out_shape=jax.ShapeDtypeStruct((nb, seq, nh * dh), BF16),
        scratch_shapes=[
            pltpu.VMEM((seq, LANES), F32),
            pltpu.VMEM((tq, tq), F32),
            pltpu.VMEM((tq, tq), F32),
        ],
        compiler_params=_params(("arbitrary", "arbitrary")),
        name="flash",
    )(q, k, vt, fcum.reshape(nb, nh, 1, seq))


def kernel(x, c, ada_w, ada_b, norm_g, mlp_w1, mlp_w2, gmlp_w_in, gmlp_ln_g, gmlp_ln_b,
           gmlp_ws, gmlp_bs, gmlp_w_out, kv_norm_g, kv_ada_w, kv_ada_b, w_kv, k_norm_g,
           w_f, b_f, attn_wq, q_norm_g, attn_wo):
    nb, seq, d = x.shape
    depth = ada_w.shape[0]
    n_a = gmlp_w_in.shape[0]
    n_heads = w_f.shape[1]

    modt = _mod(c, ada_w, ada_b).reshape(depth * nb * N_MOD, 1, d)
    kvmod = _mod(c, kv_ada_w[None], kv_ada_b[None]).reshape(nb * 2, 1, d)

    def mod_row(layer, which):
        return lambda b: (layer * nb + b) * N_MOD + which

    w_in = _cast_bf16(gmlp_w_in)
    w_out = _cast_bf16(gmlp_w_out)
    w1 = _cast_bf16(mlp_w1)
    w2 = _cast_bf16(mlp_w2)
    wkv = _cast_bf16(w_kv[None])
    wq = _cast_bf16(attn_wq)
    wo = _cast_bf16(attn_wo)

    x2 = x.reshape(nb * seq, d)
    inv_sqrt = 1.0 / float(HEAD_DIM) ** 0.5
    k_heads = v_heads = fcum = None
    for layer in range(depth):
        g1 = norm_g[layer, 0][None]
        g2 = norm_g[layer, 1][None]
        if layer < n_a:
            a = layer
            z = _proj_gelu(x2, g1, modt, mod_row(layer, 0), mod_row(layer, 1), w_in, a, seq)
            x2 = _gate_outproj(z, gmlp_ln_g[a][None], gmlp_ln_b[a][None], gmlp_ws[a],
                               gmlp_bs[a].T, w_out, a, x2, modt, mod_row(layer, 2), seq)
        else:
            if layer == n_a:
                kv_sh = lambda b: b * 2
                kv_sc = lambda b: b * 2 + 1
                kvg = kv_norm_g[None]
                w_f_pad = jnp.pad(w_f, ((0, 0), (0, LANES - n_heads))).astype(BF16)
                b_f_pad = jnp.pad(b_f, (0, LANES - n_heads))[None]
                k_heads, v_heads, logf = _proj_kvf(x2, kvg, kvmod, kv_sh, kv_sc, wkv,
                                                   k_norm_g[None], w_f_pad, b_f_pad, n_heads,
                                                   seq, FLASH_TILE)
                fcum = _cumsum(logf.reshape(nb * n_heads, seq)).reshape(nb, n_heads, seq)
            bl = layer - n_a
            q_heads = _proj_heads(x2, g1, modt, mod_row(layer, 0), mod_row(layer, 1),
                                  wq, bl, 0, d, q_norm_g[bl][None], seq,
                                  norm=True, out_scale=inv_sqrt * LOG2E)
            o = _flash(q_heads, k_heads, v_heads, fcum, tq=FLASH_TILE)
            x2 = _outproj(o.reshape(nb * seq, d), wo, bl, x2, modt, mod_row(layer, 2), seq)
        x2 = _mlp(x2, g2, modt, mod_row(layer, 3), mod_row(layer, 4), mod_row(layer, 5),
                  w1, w2, layer, seq)
    return x2.reshape(nb, seq, d)
```

```python
import functools

import jax
import jax.numpy as jnp
from jax import lax
from jax.experimental import pallas as pl
from jax.experimental.pallas import tpu as pltpu

F32 = jnp.float32
BF16 = jnp.bfloat16

EPS = 1e-6
CHUNK = 64
GMLP_BLOCK = 128
GMLP_GROUPS = 8
HEAD_DIM = 128
N_MOD = 6
LANES = 128
SUBLANES = 8
MXU_DIM = 256
LOG2E = 1.4426950408889634
FLASH_TILE = 512

MIB = 1024 * 1024
VMEM_LIMIT = 56 * MIB


def _params(semantics, vmem=VMEM_LIMIT):
    return pltpu.CompilerParams(dimension_semantics=semantics, vmem_limit_bytes=vmem)


def _tile(n, target, align=MXU_DIM):
    best = None
    for t in range(align, min(n, target) + 1, align):
        if n % t == 0:
            best = t
    assert best is not None, (n, target, align)
    return best


def _adaln(x, g, shift, scale):
    ms = jnp.mean(x * x, axis=-1, keepdims=True)
    return (x * lax.rsqrt(ms + EPS)) * (g * (1.0 + scale)) + shift


def _gelu(a):
    return 0.5 * a * (1.0 + lax.erf(a * (0.5 ** 0.5)))


def _mod_kernel(c_ref, w_ref, b_ref, o_ref):
    sc = jax.nn.silu(c_ref[...])
    o_ref[0] = jnp.dot(sc, w_ref[0], preferred_element_type=F32) + b_ref[0]


def _mod(c, w, b, tn=1024):
    nl, d, n = w.shape
    nb = c.shape[0]
    tn = _tile(n, tn)
    return pl.pallas_call(
        _mod_kernel,
        grid=(nl, n // tn),
        in_specs=[
            pl.BlockSpec((nb, d), lambda l, j: (0, 0)),
            pl.BlockSpec((1, d, tn), lambda l, j: (l, 0, j)),
            pl.BlockSpec((1, 1, tn), lambda l, j: (l, 0, j)),
        ],
        out_specs=pl.BlockSpec((1, nb, tn), lambda l, j: (l, 0, j)),
        out_shape=jax.ShapeDtypeStruct((nl, nb, n), F32),
        compiler_params=_params(("arbitrary", "arbitrary")),
        name="mod",
    )(c, w, b.reshape(nl, 1, n))


def _cast_kernel(w_ref, o_ref):
    o_ref[...] = w_ref[...].astype(o_ref.dtype)


def _cast_bf16(w, block_elems=2 * 1024 * 1024):
    nl, r, c = w.shape
    tr = _tile(r, max(MXU_DIM, block_elems // c))
    return pl.pallas_call(
        _cast_kernel,
        grid=(nl, r // tr),
        in_specs=[pl.BlockSpec((1, tr, c), lambda l, i: (l, i, 0))],
        out_specs=pl.BlockSpec((1, tr, c), lambda l, i: (l, i, 0)),
        out_shape=jax.ShapeDtypeStruct(w.shape, BF16),
        compiler_params=_params(("arbitrary", "arbitrary")),
        name="cast",
    )(w)


def _one_tile_ahead(fill, main, buf_a, buf_b):
    i = pl.program_id(0)

    @pl.when(i == 0)
    def _():
        fill(buf_a)

    @pl.when(jnp.logical_and(i > 0, i % 2 == 0))
    def _():
        main(buf_b)
        fill(buf_a)

    @pl.when(i % 2 == 1)
    def _():
        main(buf_a)
        fill(buf_b)


def _ahead(x_ref, g_ref, sh_ref, sc_ref, h_a, h_b, main, x_keep=None):
    j = pl.program_id(1)
    rows_per_step = x_ref.shape[0]

    def fill(h_dst):
        rows = pl.ds(pl.multiple_of(j * rows_per_step, rows_per_step), rows_per_step)
        x = x_ref[...]
        h_dst[rows, :] = _adaln(x, g_ref[...], sh_ref[0], sc_ref[0]).astype(BF16)
        if x_keep is not None:
            x_keep[rows, :] = x

    _one_tile_ahead(fill, main, h_a, h_b)


def _ahead_specs(d, tm, nj, n_tiles, tpb, sh_row, sc_row):
    fill_tile = lambda i: jnp.minimum(i, n_tiles - 1)
    return [
        pl.BlockSpec((tm // nj, d), lambda i, j: (fill_tile(i) * nj + j, 0)),
        pl.BlockSpec((1, d), lambda i, j: (0, 0)),
        pl.BlockSpec((1, 1, d), lambda i, j: (sh_row(fill_tile(i) // tpb), 0, 0)),
        pl.BlockSpec((1, 1, d), lambda i, j: (sc_row(fill_tile(i) // tpb), 0, 0)),
    ]


def _out_tile(i):
    return jnp.maximum(i - 1, 0)


def _col_step(i, j):
    return jnp.where(i == 0, 0, j)


def _proj_gelu_kernel(x_ref, g_ref, sh_ref, sc_ref, w_ref, o_ref, h_a, h_b):
    def main(h):
        a = jnp.dot(h[...], w_ref[...], preferred_element_type=F32)
        o_ref[...] = _gelu(a).astype(o_ref.dtype)

    _ahead(x_ref, g_ref, sh_ref, sc_ref, h_a, h_b, main)


def _proj_gelu(x2, g, modt, sh_row, sc_row, w, layer, seq, tm=1024, tn=1536):
    t, d = x2.shape
    n = w.shape[2]
    tm = _tile(seq, tm)
    tn = _tile(n, tn)
    tpb = seq // tm
    n_tiles = t // tm
    nj = n // tn
    return pl.pallas_call(
        _proj_gelu_kernel,
        grid=(n_tiles + 1, nj),
        in_specs=_ahead_specs(d, tm, nj, n_tiles, tpb, sh_row, sc_row) + [
            pl.BlockSpec((None, d, tn), lambda i, j: (layer, 0, _col_step(i, j))),
        ],
        out_specs=pl.BlockSpec((tm, tn), lambda i, j: (_out_tile(i), _col_step(i, j))),
        out_shape=jax.ShapeDtypeStruct((t, n), BF16),
        scratch_shapes=[pltpu.VMEM((tm, d), BF16), pltpu.VMEM((tm, d), BF16)],
        compiler_params=_params(("arbitrary", "arbitrary")),
        name="proj_gelu",
    )(x2, g, modt, modt, w)


def _proj_heads_kernel(x_ref, g_ref, sh_ref, sc_ref, w_ref, hg_ref, o_ref, h_a, h_b, *,
                       norm, out_scale, transpose_chunk):
    def main(h):
        a = jnp.dot(h[...], w_ref[...], preferred_element_type=F32)
        for hh in range(a.shape[1] // HEAD_DIM):
            t = a[:, hh * HEAD_DIM:(hh + 1) * HEAD_DIM]
            if norm:
                ms = jnp.mean(t * t, axis=-1, keepdims=True)
                t = t * lax.rsqrt(ms + EPS) * hg_ref[...]
            if out_scale != 1.0:
                t = t * out_scale
            if transpose_chunk:
                tt = t.T
                for cc in range(t.shape[0] // transpose_chunk):
                    cols = slice(cc * transpose_chunk, (cc + 1) * transpose_chunk)
                    o_ref[0, hh, cc] = tt[:, cols].astype(o_ref.dtype)
            else:
                o_ref[0, hh] = t.astype(o_ref.dtype)

    _ahead(x_ref, g_ref, sh_ref, sc_ref, h_a, h_b, main)


def _proj_heads(x2, g, modt, sh_row, sc_row, w, layer, col0, n, head_g, seq, *, norm,
                out_scale=1.0, transpose_chunk=0, tm=512, tn=2048):
    t, d = x2.shape
    tm = _tile(seq, tm)
    tn = _tile(n, tn)
    nb = t // seq
    nh = n // HEAD_DIM
    tpb = seq // tm
    n_tiles = t // tm
    nj = n // tn
    col0_blocks = col0 // tn
    assert col0 % tn == 0
    hpt = tn // HEAD_DIM
    kern = functools.partial(_proj_heads_kernel, norm=norm, out_scale=out_scale,
                             transpose_chunk=transpose_chunk)
    if transpose_chunk:
        cpt = tm // transpose_chunk
        out_spec = pl.BlockSpec(
            (1, hpt, cpt, HEAD_DIM, transpose_chunk),
            lambda i, j: (_out_tile(i) // tpb, _col_step(i, j), _out_tile(i) % tpb, 0, 0))
        out_shape = (nb, nh, seq // transpose_chunk, HEAD_DIM, transpose_chunk)
    else:
        out_spec = pl.BlockSpec(
            (1, hpt, tm, HEAD_DIM),
            lambda i, j: (_out_tile(i) // tpb, _col_step(i, j), _out_tile(i) % tpb, 0))
        out_shape = (nb, nh, seq, HEAD_DIM)
    return pl.pallas_call(
        kern,
        grid=(n_tiles + 1, nj),
        in_specs=_ahead_specs(d, tm, nj, n_tiles, tpb, sh_row, sc_row) + [
            pl.BlockSpec((None, d, tn), lambda i, j: (layer, 0, col0_blocks + _col_step(i, j))),
            pl.BlockSpec((1, HEAD_DIM), lambda i, j: (0, 0)),
        ],
        out_specs=out_spec,
        out_shape=jax.ShapeDtypeStruct(out_shape, BF16),
        scratch_shapes=[pltpu.VMEM((tm, d), BF16), pltpu.VMEM((tm, d), BF16)],
        compiler_params=_params(("arbitrary", "arbitrary")),
        name="proj_heads",
    )(x2, g, modt, modt, w, head_g)


def _proj_kvf_kernel(x_ref, g_ref, sh_ref, sc_ref, w_ref, hg_ref, wf_ref, bf_ref,
                     k_ref, v_ref, f_ref, h_a, h_b, *, transpose_chunk, n_heads):
    j = pl.program_id(1)

    def main(h):
        @pl.when(j == 0)
        def _():
            a = jnp.dot(h[...], w_ref[...], preferred_element_type=F32)
            for hh in range(a.shape[1] // HEAD_DIM):
                t = a[:, hh * HEAD_DIM:(hh + 1) * HEAD_DIM]
                ms = jnp.mean(t * t, axis=-1, keepdims=True)
                k_ref[0, hh] = (t * lax.rsqrt(ms + EPS) * hg_ref[...]).astype(k_ref.dtype)

        @pl.when(j == 1)
        def _():
            f = jnp.dot(h[...], wf_ref[...], preferred_element_type=F32) + bf_ref[...]
            f_ref[0] = jax.nn.log_sigmoid(f).T[:n_heads]
            a = jnp.dot(h[...], w_ref[...], preferred_element_type=F32)
            for hh in range(a.shape[1] // HEAD_DIM):
                tt = a[:, hh * HEAD_DIM:(hh + 1) * HEAD_DIM].T
                for cc in range(tt.shape[1] // transpose_chunk):
                    cols = slice(cc * transpose_chunk, (cc + 1) * transpose_chunk)
                    v_ref[0, hh, cc] = tt[:, cols].astype(v_ref.dtype)

    _ahead(x_ref, g_ref, sh_ref, sc_ref, h_a, h_b, main)


def _proj_kvf(x2, g, modt, sh_row, sc_row, w, head_g, w_f_pad, b_f_pad, n_heads, seq,
              transpose_chunk, tm=512):
    t, d = x2.shape
    n = w.shape[2] // 2
    tm = _tile(seq, tm)
    nb = t // seq
    nh = n // HEAD_DIM
    tpb = seq // tm
    n_tiles = t // tm
    nj = 2
    cpt = tm // transpose_chunk
    tile_pos = lambda i: (_out_tile(i) // tpb, _out_tile(i) % tpb)
    kern = functools.partial(_proj_kvf_kernel, transpose_chunk=transpose_chunk, n_heads=n_heads)
    return pl.pallas_call(
        kern,
        grid=(n_tiles + 1, nj),
        in_specs=_ahead_specs(d, tm, nj, n_tiles, tpb, sh_row, sc_row) + [
            pl.BlockSpec((None, d, n), lambda i, j: (0, 0, _col_step(i, j))),
            pl.BlockSpec((1, HEAD_DIM), lambda i, j: (0, 0)),
            pl.BlockSpec((d, LANES), lambda i, j: (0, 0)),
            pl.BlockSpec((1, LANES), lambda i, j: (0, 0)),
        ],
        out_specs=[
            pl.BlockSpec((1, nh, tm, HEAD_DIM),
                         lambda i, j: (tile_pos(i)[0], 0, tile_pos(i)[1], 0)),
            pl.BlockSpec((1, nh, cpt, HEAD_DIM, transpose_chunk),
                         lambda i, j: (tile_pos(i)[0], 0, tile_pos(i)[1], 0, 0)),
            pl.BlockSpec((1, n_heads, tm), lambda i, j: (tile_pos(i)[0], 0, tile_pos(i)[1])),
        ],
        out_shape=[
            jax.ShapeDtypeStruct((nb, nh, seq, HEAD_DIM), BF16),
            jax.ShapeDtypeStruct((nb, nh, seq // transpose_chunk, HEAD_DIM, transpose_chunk), BF16),
            jax.ShapeDtypeStruct((nb, n_heads, seq), F32),
        ],
        scratch_shapes=[pltpu.VMEM((tm, d), BF16), pltpu.VMEM((tm, d), BF16)],
        compiler_params=_params(("arbitrary", "arbitrary")),
        name="proj_kvf",
    )(x2, g, modt, modt, w, head_g, w_f_pad, b_f_pad)


def _gate_outproj_kernel(u_ref, v_ref, lg_ref, lb_ref, ws_ref, bst_ref, w_ref, x_ref, gate_ref,
                         o_ref, vn_scr, p_a, p_b):
    k = pl.program_id(1)
    nk, _, tk = p_a.shape
    half = v_ref.shape[1]
    gdim = half // GMLP_GROUPS

    @pl.when(k == 0)
    def _():
        o_ref[...] = x_ref[...]

    def layer_norm_v():
        v = v_ref[...].astype(F32)
        mu = jnp.mean(v, axis=-1, keepdims=True)
        vc = v - mu
        var = jnp.mean(vc * vc, axis=-1, keepdims=True)
        vn_scr[...] = (vc * lax.rsqrt(var + EPS) * lg_ref[...] + lb_ref[...]).astype(BF16)

    def gate_group(p_dst, gi):
        row = lax.broadcasted_iota(jnp.int32, (GMLP_BLOCK, GMLP_BLOCK), 0) // CHUNK
        col = lax.broadcasted_iota(jnp.int32, (GMLP_BLOCK, GMLP_BLOCK), 1) // CHUNK
        ws = jnp.where(col <= row, ws_ref[gi], 0.0).astype(BF16)
        cols = slice(gi * gdim, (gi + 1) * gdim)
        sv = jnp.dot(ws, vn_scr[:, cols], preferred_element_type=F32) + bst_ref[:, gi:gi + 1]
        p = (u_ref[:, cols].astype(F32) * sv).astype(BF16)
        rows = pl.ds(pl.multiple_of(k * GMLP_BLOCK, GMLP_BLOCK), GMLP_BLOCK)
        k_slice, off = divmod(gi * gdim, tk)
        p_dst[k_slice, rows, off:off + gdim] = p

    def fill(p_dst):
        layer_norm_v()
        for gi in range(GMLP_GROUPS):
            gate_group(p_dst, gi)

    def steady(p_src, p_dst):
        layer_norm_v()
        n_pieces = GMLP_GROUPS // 2
        piece = o_ref.shape[1] // n_pieces
        for pi in range(n_pieces):
            cols = slice(pi * piece, (pi + 1) * piece)
            d = jnp.dot(p_src[k], w_ref[:, cols], preferred_element_type=F32)
            o_ref[:, cols] += gate_ref[0][:, cols] * d
            gate_group(p_dst, 2 * pi)
            gate_group(p_dst, 2 * pi + 1)

    i = pl.program_id(0)

    @pl.when(i == 0)
    def _():
        fill(p_a)

    @pl.when(jnp.logical_and(i > 0, i % 2 == 0))
    def _():
        steady(p_b, p_a)

    @pl.when(i % 2 == 1)
    def _():
        steady(p_a, p_b)


def _gate_outproj(z, ln_g, ln_b, ws, bs_t, w, layer, x2, modt, gate_row, seq, tk=1536):
    t, n2 = z.shape
    half = n2 // 2
    d = w.shape[2]
    tk = _tile(half, tk)
    nk = half // tk
    tm = nk * GMLP_BLOCK
    assert seq % tm == 0 and tk % (half // GMLP_GROUPS) == 0
    tpb = seq // tm
    n_tiles = t // tm
    fill_block = lambda i, k: jnp.minimum(i, n_tiles - 1) * nk + k
    return pl.pallas_call(
        _gate_outproj_kernel,
        grid=(n_tiles + 1, nk),
        in_specs=[
            pl.BlockSpec((GMLP_BLOCK, half), lambda i, k: (fill_block(i, k), 0)),
            pl.BlockSpec((GMLP_BLOCK, half), lambda i, k: (fill_block(i, k), 1)),
            pl.BlockSpec((1, half), lambda i, k: (0, 0)),
            pl.BlockSpec((1, half), lambda i, k: (0, 0)),
            pl.BlockSpec(ws.shape, lambda i, k: (0, 0, 0)),
            pl.BlockSpec(bs_t.shape, lambda i, k: (0, 0)),
            pl.BlockSpec((None, tk, d), lambda i, k: (layer, _col_step(i, k), 0)),
            pl.BlockSpec((tm, d), lambda i, k: (_out_tile(i), 0)),
            pl.BlockSpec((1, 1, d), lambda i, k: (gate_row(_out_tile(i) // tpb), 0, 0)),
        ],
        out_specs=pl.BlockSpec((tm, d), lambda i, k: (_out_tile(i), 0)),
        out_shape=jax.ShapeDtypeStruct((t, d), F32),
        scratch_shapes=[
            pltpu.VMEM((GMLP_BLOCK, half), BF16),
            pltpu.VMEM((nk, tm, tk), BF16),
            pltpu.VMEM((nk, tm, tk), BF16),
        ],
        compiler_params=_params(("arbitrary", "arbitrary")),
        name="gate_outproj",
    )(z, z, ln_g, ln_b, ws, bs_t, w, x2, modt)


def _outproj_kernel(a_ref, w_ref, x_ref, gate_ref, o_ref, *, nk):
    if nk == 1:
        d = jnp.dot(a_ref[...], w_ref[...], preferred_element_type=F32)
        o_ref[...] = x_ref[...] + gate_ref[0] * d
        return

    @pl.when(pl.program_id(1) == 0)
    def _():
        o_ref[...] = x_ref[...]

    d = jnp.dot(a_ref[...], w_ref[...], preferred_element_type=F32)
    o_ref[...] += gate_ref[0] * d


def _outproj(a, w, layer, x2, modt, gate_row, seq, tm=512, tk=2048):
    t, kdim = a.shape
    d = w.shape[2]
    tm = _tile(seq, tm)
    tk = _tile(kdim, tk)
    tpb = seq // tm
    nk = kdim // tk
    return pl.pallas_call(
        functools.partial(_outproj_kernel, nk=nk),
        grid=(t // tm, nk),
        in_specs=[
            pl.BlockSpec((tm, tk), lambda i, k: (i, k)),
            pl.BlockSpec((None, tk, d), lambda i, k: (layer, k, 0)),
            pl.BlockSpec((tm, d), lambda i, k: (i, 0)),
            pl.BlockSpec((1, 1, d), lambda i, k: (gate_row(i // tpb), 0, 0)),
        ],
        out_specs=pl.BlockSpec((tm, d), lambda i, k: (i, 0)),
        out_shape=jax.ShapeDtypeStruct((t, d), F32),
        compiler_params=_params(("arbitrary", "arbitrary")),
        name="outproj",
    )(a, w, x2, modt)


def _mlp_kernel(x_ref, g_ref, sh_ref, sc_ref, gate_ref, w1_ref, w2_ref, o_ref,
                h_a, h_b, x_keep):
    @pl.when(jnp.logical_and(pl.program_id(0) > 0, pl.program_id(1) == 0))
    def _():
        o_ref[...] = x_keep[...]

    def main(h):
        a = jnp.dot(h[...], w1_ref[...], preferred_element_type=F32)
        a = jnp.square(jnp.maximum(a, 0.0)).astype(BF16)
        o_ref[...] += gate_ref[0] * jnp.dot(a, w2_ref[...], preferred_element_type=F32)

    _ahead(x_ref, g_ref, sh_ref, sc_ref, h_a, h_b, main, x_keep=x_keep)


def _mlp(x2, g, modt, sh_row, sc_row, gate_row, w1, w2, layer, seq, tm=1024, tf=1024):
    t, d = x2.shape
    dff = w1.shape[2]
    tm = _tile(seq, tm)
    tf = _tile(dff, tf)
    tpb = seq // tm
    n_tiles = t // tm
    nj = dff // tf
    return pl.pallas_call(
        _mlp_kernel,
        grid=(n_tiles + 1, nj),
        in_specs=_ahead_specs(d, tm, nj, n_tiles, tpb, sh_row, sc_row) + [
            pl.BlockSpec((1, 1, d), lambda i, j: (gate_row(_out_tile(i) // tpb), 0, 0)),
            pl.BlockSpec((None, d, tf), lambda i, j: (layer, 0, _col_step(i, j))),
            pl.BlockSpec((None, tf, d), lambda i, j: (layer, _col_step(i, j), 0)),
        ],
        out_specs=pl.BlockSpec((tm, d), lambda i, j: (_out_tile(i), 0)),
        out_shape=jax.ShapeDtypeStruct((t, d), F32),
        scratch_shapes=[pltpu.VMEM((tm, d), BF16), pltpu.VMEM((tm, d), BF16),
                        pltpu.VMEM((tm, d), F32)],
        compiler_params=_params(("arbitrary", "arbitrary")),
        name="mlp",
    )(x2, g, modt, modt, modt, w1, w2)


def _cumsum_kernel(x_ref, o_ref):
    x = x_ref[...]
    n = x.shape[1]
    lane = lax.broadcasted_iota(jnp.int32, x.shape, 1)
    d = 1
    while d < n:
        x = x + jnp.where(lane >= d, pltpu.roll(x, d, axis=1), 0.0)
        d *= 2
    o_ref[...] = x


def _cumsum(x):
    return pl.pallas_call(
        _cumsum_kernel,
        out_shape=jax.ShapeDtypeStruct(x.shape, x.dtype),
        name="cumsum",
    )(x)


def _flash_kernel(q_ref, k_ref, vt_ref, f_ref, o_ref, frep_scr, sa_scr, sb_scr, *, tq):
    seq = k_ref.shape[2]
    tk = tq
    nq = seq // tq
    n_lane_groups = tq // LANES
    nt = (((1,), (1,)), ((), ()))

    f = f_ref[0, 0]
    for c in range(nq):
        fc = jnp.broadcast_to(f[:, c * tk:(c + 1) * tk], (LANES, tk))
        frep_scr[c * tk:(c + 1) * tk, :] = fc.T * LOG2E

    ones = jnp.ones((2 * SUBLANES, tk), BF16)
    key = lax.broadcasted_iota(jnp.int32, (tk, tq), 0)
    qry = lax.broadcasted_iota(jnp.int32, (tk, tq), 1)
    visible = key <= qry

    def scores(qi, c):
        q = q_ref[0, 0, qi * tq:(qi + 1) * tq, :]
        s = lax.dot_general(k_ref[0, 0, c * tk:(c + 1) * tk, :], q, nt,
                            preferred_element_type=F32)
        bias = frep_scr[qi * tq:qi * tq + 1, :] - frep_scr[c * tk:(c + 1) * tk, :]
        s = s + jnp.concatenate([bias] * n_lane_groups, axis=1)
        s = jnp.where(visible, s, -jnp.inf) if c == qi else s
        return s, jnp.max(s, axis=0, keepdims=True)

    steps = [(qi, c) for qi in range(nq) for c in range(qi + 1)]
    bufs = (sa_scr, sb_scr)
    bufs[0][...], m_next = scores(*steps[0])
    m = acc = None
    for g, (qi, c) in enumerate(steps):
        m_chunk = m_next
        if g + 1 < len(steps):
            bufs[(g + 1) % 2][...], m_next = scores(*steps[g + 1])
        s_ref = bufs[g % 2]
        v_aug = jnp.concatenate([vt_ref[0, 0, c], ones], axis=0)
        if c == 0:
            m = m_chunk
            p = jnp.exp2((s_ref[...] - m).astype(BF16))
            acc = jnp.dot(v_aug, p, preferred_element_type=F32)
        else:
            m_new = jnp.maximum(m, m_chunk)
            p = jnp.exp2((s_ref[...] - m_new).astype(BF16))
            acc = jnp.exp2(m - m_new) * acc + jnp.dot(v_aug, p, preferred_element_type=F32)
            m = m_new
        if c == qi:
            inv_l = 1.0 / acc[HEAD_DIM:HEAD_DIM + 1]
            o_ref[0, qi * tq:(qi + 1) * tq, :] = (acc[:HEAD_DIM] * inv_l).T.astype(o_ref.dtype)


def _flash(q, k, vt, fcum, tq):
    nb, nh, seq, dh = q.shape
    nq = seq // tq
    assert vt.shape == (nb, nh, nq, dh, tq)
    return pl.pallas_call(
        functools.partial(_flash_kernel, tq=tq),
        grid=(nb, nh),
        in_specs=[
            pl.BlockSpec((1, 1, seq, dh), lambda b, h: (b, h, 0, 0)),
            pl.BlockSpec((1, 1, seq, dh), lambda b, h: (b, h, 0, 0)),
            pl.BlockSpec((1, 1, nq, dh, tq), lambda b, h: (b, h, 0, 0, 0)),
            pl.BlockSpec((1, 1, 1, seq), lambda b, h: (b, h, 0, 0)),
        ],
        out_specs=pl.BlockSpec((1, seq, dh), lambda b, h: (b, 0, h)),
        out_shape=jax.ShapeDtypeStruct((nb, seq, nh * dh), BF16),
        scratch_shapes=[
            pltpu.VMEM((seq, LANES), F32),
            pltpu.VMEM((tq, tq), F32),
            pltpu.VMEM((tq, tq), F32),
        ],
        compiler_params=_params(("arbitrary", "arbitrary")),
        name="flash",
    )(q, k, vt, fcum.reshape(nb, nh, 1, seq))


def kernel(x, c, ada_w, ada_b, norm_g, mlp_w1, mlp_w2, gmlp_w_in, gmlp_ln_g, gmlp_ln_b,
           gmlp_ws, gmlp_bs, gmlp_w_out, kv_norm_g, kv_ada_w, kv_ada_b, w_kv, k_norm_g,
           w_f, b_f, attn_wq, q_norm_g, attn_wo):
    nb, seq, d = x.shape
    depth = ada_w.shape[0]
    n_a = gmlp_w_in.shape[0]
    n_heads = w_f.shape[1]

    modt = _mod(c, ada_w, ada_b).reshape(depth * nb * N_MOD, 1, d)
    kvmod = _mod(c, kv_ada_w[None], kv_ada_b[None]).reshape(nb * 2, 1, d)

    def mod_row(layer, which):
        return lambda b: (layer * nb + b) * N_MOD + which

    w_in = _cast_bf16(gmlp_w_in)
    w_out = _cast_bf16(gmlp_w_out)
    w1 = _cast_bf16(mlp_w1)
    w2 = _cast_bf16(mlp_w2)
    wkv = _cast_bf16(w_kv[None])
    wq = _cast_bf16(attn_wq)
    wo = _cast_bf16(attn_wo)

    x2 = x.reshape(nb * seq, d)
    inv_sqrt = 1.0 / float(HEAD_DIM) ** 0.5
    k_heads = v_heads = fcum = None
    for layer in range(depth):
        g1 = norm_g[layer, 0][None]
        g2 = norm_g[layer, 1][None]
        if layer < n_a:
            a = layer
            z = _proj_gelu(x2, g1, modt, mod_row(layer, 0), mod_row(layer, 1), w_in, a, seq)
            x2 = _gate_outproj(z, gmlp_ln_g[a][None], gmlp_ln_b[a][None], gmlp_ws[a],
                               gmlp_bs[a].T, w_out, a, x2, modt, mod_row(layer, 2), seq)
        else:
            if layer == n_a:
                kv_sh = lambda b: b * 2
                kv_sc = lambda b: b * 2 + 1
                kvg = kv_norm_g[None]
                w_f_pad = jnp.pad(w_f, ((0, 0), (0, LANES - n_heads))).astype(BF16)
                b_f_pad = jnp.pad(b_f, (0, LANES - n_heads))[None]
                k_heads, v_heads, logf = _proj_kvf(x2, kvg, kvmod, kv_sh, kv_sc, wkv,
                                                   k_norm_g[None], w_f_pad, b_f_pad, n_heads,
                                                   seq, FLASH_TILE)
                fcum = _cumsum(logf.reshape(nb * n_heads, seq)).reshape(nb, n_heads, seq)
            bl = layer - n_a
            q_heads = _proj_heads(x2, g1, modt, mod_row(layer, 0), mod_row(layer, 1),
                                  wq, bl, 0, d, q_norm_g[bl][None], seq,
                                  norm=True, out_scale=inv_sqrt * LOG2E)
            o = _flash(q_heads, k_heads, v_heads, fcum, tq=FLASH_TILE)
            x2 = _outproj(o.reshape(nb * seq, d), wo, bl, x2, modt, mod_row(layer, 2), seq)
        x2 = _mlp(x2, g2, modt, mod_row(layer, 3), mod_row(layer, 4), mod_row(layer, 5),
                  w1, w2, layer, seq)
    return x2.reshape(nb, seq, d)
```
